```python
import jax, jax.numpy as jnp
from jax import lax
import numpy as np

D_MODEL = 1024
BATCH = 8
SEQ = 4096
DEPTH = 2

PLE_DIM = 256
D_FF = 2752
M_HEADS = 4
M_HEAD_DIM = 256
M_WIDTH = M_HEADS * M_HEAD_DIM
M_CHUNK = 64
CONV_K = 4
A_Q_HEADS = 16
A_KV_HEADS = 4
A_HEAD_DIM = 64
A_GROUP = A_Q_HEADS // A_KV_HEADS
A_WIDTH = A_Q_HEADS * A_HEAD_DIM
A_KV_WIDTH = A_KV_HEADS * A_HEAD_DIM
WINDOW = 128
A_BLOCK = 128
EPS = 1e-6
N_IN = 4 * M_WIDTH + 2 * M_HEADS + A_WIDTH + 2 * A_KV_WIDTH + 2 * D_MODEL

kernel_name = "hybrid_mlstm_swa_macaron_ple"


def rmsnorm(x, g):
    xf = x.astype(jnp.float32)
    y = xf * lax.rsqrt(jnp.mean(xf * xf, axis=-1, keepdims=True) + EPS)
    return (y * g.astype(jnp.float32)).astype(x.dtype)


def head_rmsnorm(x, g):
    xf = x.astype(jnp.float32)
    y = xf * lax.rsqrt(jnp.mean(xf * xf, axis=-1, keepdims=True) + EPS)
    return y * g.astype(jnp.float32)


def swiglu(x, w_gate, w_up, w_down):
    return (jax.nn.silu(x @ w_gate) * (x @ w_up)) @ w_down


def causal_dwconv(x, w, b):
    c = x.shape[-1]
    y = lax.conv_general_dilated(x, w[:, None, :].astype(x.dtype), window_strides=(1,),
                                 padding=[(CONV_K - 1, 0)],
                                 dimension_numbers=("NWC", "WIO", "NWC"),
                                 feature_group_count=c)
    return y + b


def mlstm_chunkwise(q, k, v, ig, lf):
    bsz, t, nh, dh = q.shape
    nc = t // M_CHUNK

    def to_chunks(z):
        return z.reshape(bsz, nc, M_CHUNK, nh, dh).transpose(1, 0, 3, 2, 4)

    def g_chunks(z):
        return z.reshape(bsz, nc, M_CHUNK, nh).transpose(1, 0, 3, 2)

    causal = jnp.tril(jnp.ones((M_CHUNK, M_CHUNK), dtype=bool))

    def step(carry, xs):
        c_st, n_st, m_st = carry
        qc, kc, vc, ic, fc = xs
        b = jnp.cumsum(fc, axis=-1)
        dlog = jnp.where(causal, b[..., :, None] - b[..., None, :] + ic[..., None, :], -jnp.inf)
        inter = b + m_st[..., None]
        m_t = jnp.maximum(inter, jnp.max(dlog, axis=-1))
        w_intra = jnp.exp(dlog - m_t[..., None])
        w_inter = jnp.exp(inter - m_t)
        s = jnp.einsum("bhtk,bhsk->bhts", qc, kc) * w_intra
        num = jnp.einsum("bhts,bhsv->bhtv", s, vc) + w_inter[..., None] * jnp.einsum("bhtk,bhvk->bhtv", qc, c_st)
        den = jnp.sum(s, axis=-1) + w_inter * jnp.einsum("bhtk,bhk->bht", qc, n_st)
        h = num / jnp.maximum(jnp.abs(den), jnp.exp(-m_t))[..., None]
        b_last = b[..., -1]
        a = b_last[..., None] - b + ic
        m_new = jnp.maximum(b_last + m_st, jnp.max(a, axis=-1))
        w_state = jnp.exp(a - m_new[..., None])
        decay = jnp.exp(b_last + m_st - m_new)
        c_new = decay[..., None, None] * c_st + jnp.einsum("bhsv,bhsk->bhvk", vc * w_state[..., None], kc)
        n_new = decay[..., None] * n_st + jnp.einsum("bhs,bhsk->bhk", w_state, kc)
        return (c_new, n_new, m_new), h

    init = (jnp.zeros((bsz, nh, dh, dh), jnp.float32),
            jnp.zeros((bsz, nh, dh), jnp.float32),
            jnp.zeros((bsz, nh), jnp.float32))
    xs = (to_chunks(q), to_chunks(k), to_chunks(v), g_chunks(ig), g_chunks(lf))
    _, hs = lax.scan(step, init, xs)
    return hs.transpose(1, 0, 3, 2, 4).reshape(bsz, t, nh, dh)


def swa_gqa_sinks(q, k, v, sinks):
    bsz, t = q.shape[0], q.shape[1]
    nb = t // A_BLOCK
    qb = q.reshape(bsz, nb, A_BLOCK, A_KV_HEADS, A_GROUP, A_HEAD_DIM)

    def windows(z):
        zp = jnp.pad(z, ((0, 0), (A_BLOCK, 0), (0, 0), (0, 0)))
        zb = zp.reshape(bsz, nb + 1, A_BLOCK, A_KV_HEADS, A_HEAD_DIM)
        return jnp.concatenate([zb[:, :-1], zb[:, 1:]], axis=2)

    kw, vw = windows(k), windows(v)
    scores = jnp.einsum("bnqhgd,bnkhd->bnhgqk", qb, kw).astype(jnp.float32) * (A_HEAD_DIM ** -0.5)
    qi = jnp.arange(A_BLOCK)[:, None]
    kj = jnp.arange(2 * A_BLOCK)[None, :]
    rel = A_BLOCK + qi - kj
    band = (rel >= 0) & (rel < WINDOW)
    key_pos = jnp.arange(nb)[:, None, None] * A_BLOCK + kj[None] - A_BLOCK
    valid = band[None] & (key_pos >= 0)
    scores = jnp.where(valid[None, :, None, None], scores, -jnp.inf)
    sink = sinks.astype(jnp.float32).reshape(A_KV_HEADS, A_GROUP)[None, None, :, :, None, None]
    mx = jnp.maximum(jnp.max(scores, axis=-1, keepdims=True), sink)
    pr = jnp.exp(scores - mx)
    den = jnp.sum(pr, axis=-1, keepdims=True) + jnp.exp(sink - mx)
    attn = (pr / den).astype(v.dtype)
    out = jnp.einsum("bnhgqk,bnkhd->bnqhgd", attn, vw)
    return out.reshape(bsz, t, A_WIDTH)


def hybrid_layer(h, p_i, ffn1_norm, ffn1_w_gate, ffn1_w_up, ffn1_w_down, mix_norm, w_in,
                 m_conv_w, m_conv_b, m_igate_b, m_fgate_b, m_out_norm, a_q_norm, a_k_norm,
                 a_sinks, w_branch_m, w_branch_a, w_out, ffn2_norm, ffn2_w_gate, ffn2_w_up,
                 ffn2_w_down, ple_norm, ple_gate_w, ple_proj_w):
    bsz, t, _ = h.shape
    h = h + 0.5 * swiglu(rmsnorm(h, ffn1_norm), ffn1_w_gate, ffn1_w_up, ffn1_w_down)

    u = rmsnorm(h, mix_norm)
    z = u @ w_in
    sizes = [2 * M_WIDTH, M_WIDTH, M_WIDTH, M_HEADS, M_HEADS, A_WIDTH, A_KV_WIDTH, A_KV_WIDTH, D_MODEL, D_MODEL]
    offs = [int(o) for o in np.cumsum(sizes)[:-1]]
    m_qk, m_v, m_o, m_i, m_f, a_q, a_k, a_v, g_m, g_a = jnp.split(z, offs, axis=-1)

    m_qk = jax.nn.silu(causal_dwconv(m_qk, m_conv_w, m_conv_b))
    m_q, m_k = jnp.split(m_qk, 2, axis=-1)
    mq = m_q.astype(jnp.float32).reshape(bsz, t, M_HEADS, M_HEAD_DIM) * (M_HEAD_DIM ** -0.5)
    mk = m_k.astype(jnp.float32).reshape(bsz, t, M_HEADS, M_HEAD_DIM)
    mv = m_v.astype(jnp.float32).reshape(bsz, t, M_HEADS, M_HEAD_DIM)
    ig = m_i.astype(jnp.float32) + m_igate_b.astype(jnp.float32)
    lf = jax.nn.log_sigmoid(m_f.astype(jnp.float32) + m_fgate_b.astype(jnp.float32))
    hm = mlstm_chunkwise(mq, mk, mv, ig, lf)
    hm = head_rmsnorm(hm, m_out_norm.reshape(M_HEADS, M_HEAD_DIM)).reshape(bsz, t, M_WIDTH)
    hm = (jax.nn.sigmoid(m_o.astype(jnp.float32)) * hm).astype(h.dtype)

    aq = head_rmsnorm(a_q.reshape(bsz, t, A_Q_HEADS, A_HEAD_DIM), a_q_norm).astype(h.dtype)
    ak = head_rmsnorm(a_k.reshape(bsz, t, A_KV_HEADS, A_HEAD_DIM), a_k_norm).astype(h.dtype)
    av = a_v.reshape(bsz, t, A_KV_HEADS, A_HEAD_DIM)
    ha = swa_gqa_sinks(aq, ak, av, a_sinks)

    merged = jax.nn.sigmoid(g_m) * (hm @ w_branch_m) + jax.nn.sigmoid(g_a) * (ha @ w_branch_a)
    h = h + merged @ w_out

    h = h + 0.5 * swiglu(rmsnorm(h, ffn2_norm), ffn2_w_gate, ffn2_w_up, ffn2_w_down)

    h = h + jax.nn.sigmoid(rmsnorm(h, ple_norm) @ ple_gate_w) * (p_i @ ple_proj_w)
    return h


def setup_inputs(seed: int = 0) -> dict:
    key = jax.random.key(seed)
    ks = jax.random.split(key, 32)

    def w(k, fan_in, fan_out, scale=1.0):
        return jax.random.normal(k, (DEPTH, fan_in, fan_out), jnp.float32) * (scale * fan_in ** -0.5)

    def gain(k, n):
        return 1.0 + 0.05 * jax.random.normal(k, (DEPTH, n), jnp.float32)

    f_bias = jnp.linspace(3.0, 6.0, M_HEADS, dtype=jnp.float32)[None, :] + 0.1 * jax.random.normal(ks[10], (DEPTH, M_HEADS), jnp.float32)
    return {
        "x": jax.random.normal(ks[0], (BATCH, SEQ, D_MODEL), jnp.float32),
        "p": jax.random.normal(ks[1], (DEPTH, BATCH, SEQ, PLE_DIM), jnp.float32),
        "ffn1_norm": gain(ks[2], D_MODEL),
        "ffn1_w_gate": w(ks[3], D_MODEL, D_FF),
        "ffn1_w_up": w(ks[4], D_MODEL, D_FF),
        "ffn1_w_down": w(ks[5], D_FF, D_MODEL, 0.5),
        "mix_norm": gain(ks[6], D_MODEL),
        "w_in": w(ks[7], D_MODEL, N_IN),
        "m_conv_w": jax.random.normal(ks[8], (DEPTH, CONV_K, 2 * M_WIDTH), jnp.float32) * (CONV_K ** -0.5),
        "m_conv_b": 0.02 * jax.random.normal(ks[9], (DEPTH, 2 * M_WIDTH), jnp.float32),
        "m_igate_b": 0.1 * jax.random.normal(ks[11], (DEPTH, M_HEADS), jnp.float32),
        "m_fgate_b": f_bias,
        "m_out_norm": gain(ks[12], M_WIDTH),
        "a_q_norm": gain(ks[13], A_HEAD_DIM),
        "a_k_norm": gain(ks[14], A_HEAD_DIM),
        "a_sinks": 0.5 * jax.random.normal(ks[15], (DEPTH, A_Q_HEADS), jnp.float32),
        "w_branch_m": w(ks[16], M_WIDTH, D_MODEL),
        "w_branch_a": w(ks[17], A_WIDTH, D_MODEL),
        "w_out": w(ks[18], D_MODEL, D_MODEL, 0.5),
        "ffn2_norm": gain(ks[19], D_MODEL),
        "ffn2_w_gate": w(ks[20], D_MODEL, D_FF),
        "ffn2_w_up": w(ks[21], D_MODEL, D_FF),
        "ffn2_w_down": w(ks[22], D_FF, D_MODEL, 0.5),
        "ple_norm": gain(ks[23], D_MODEL),
        "ple_gate_w": w(ks[24], D_MODEL, D_MODEL),
        "ple_proj_w": w(ks[25], PLE_DIM, D_MODEL, 0.5),
    }


def reference(x, p, ffn1_norm, ffn1_w_gate, ffn1_w_up, ffn1_w_down, mix_norm, w_in,
              m_conv_w, m_conv_b, m_igate_b, m_fgate_b, m_out_norm, a_q_norm, a_k_norm,
              a_sinks, w_branch_m, w_branch_a, w_out, ffn2_norm, ffn2_w_gate, ffn2_w_up,
              ffn2_w_down, ple_norm, ple_gate_w, ple_proj_w):
    h = x
    for i in range(DEPTH):
        h = hybrid_layer(h, p[i], ffn1_norm[i], ffn1_w_gate[i], ffn1_w_up[i], ffn1_w_down[i],
                         mix_norm[i], w_in[i], m_conv_w[i], m_conv_b[i], m_igate_b[i],
                         m_fgate_b[i], m_out_norm[i], a_q_norm[i], a_k_norm[i], a_sinks[i],
                         w_branch_m[i], w_branch_a[i], w_out[i], ffn2_norm[i], ffn2_w_gate[i],
                         ffn2_w_up[i], ffn2_w_down[i], ple_norm[i], ple_gate_w[i], ple_proj_w[i])
    return h
```

```python
import functools

import jax
import jax.numpy as jnp
import numpy as np
from jax import lax
from jax.experimental import pallas as pl
from jax.experimental.pallas import tpu as pltpu

D_MODEL = 1024
BATCH = 8
SEQ = 4096
DEPTH = 2
PLE_DIM = 256
D_FF = 2752
M_HEADS = 4
M_HEAD_DIM = 256
M_WIDTH = M_HEADS * M_HEAD_DIM
CONV_K = 4
A_Q_HEADS = 16
A_KV_HEADS = 4
A_HEAD_DIM = 64
A_GROUP = A_Q_HEADS // A_KV_HEADS
A_WIDTH = A_Q_HEADS * A_HEAD_DIM
A_KV_WIDTH = A_KV_HEADS * A_HEAD_DIM
WINDOW = 128
A_BLOCK = 128
EPS = 1e-6

N_TOK = BATCH * SEQ
LANES = 128
SUBLANES = 8
MXU_DIM = 256
D_FF_PAD = -(-D_FF // MXU_DIM) * MXU_DIM
TM = 512
M_CHUNK = 256
GATE_ROWS = 16
VMEM_LIMIT = 56 * 1024 * 1024

BF16 = jnp.bfloat16
F32 = jnp.float32


def _const_spec(shape):
    zeros = (0,) * len(shape)
    return pl.BlockSpec(shape, lambda *_: zeros, pipeline_mode=pl.Buffered(1))


def _rms_scale(x):
    return lax.rsqrt(jnp.mean(x * x, axis=-1, keepdims=True) + EPS)


def _log_sigmoid(x):
    return jnp.minimum(x, 0.0) - jnp.log1p(jnp.exp(-jnp.abs(x)))


def _segment_cumsum(x, axis, seg):
    pos = lax.broadcasted_iota(jnp.int32, x.shape, axis) % seg
    shift = 1
    while shift < seg:
        x = x + jnp.where(pos >= shift, pltpu.roll(x, shift, axis), 0.0)
        shift *= 2
    return x


def _ffn_body(x, g_ref, wg_ref, wu_ref, wd_ref):
    xn = (x * _rms_scale(x) * g_ref[...]).astype(BF16)
    y = jnp.zeros(x.shape, F32)
    for c in range(D_FF_PAD // MXU_DIM):
        cols = slice(c * MXU_DIM, (c + 1) * MXU_DIM)
        gate = jnp.dot(xn, wg_ref[:, cols], preferred_element_type=F32)
        up = jnp.dot(xn, wu_ref[:, cols], preferred_element_type=F32)
        act = (gate * jax.nn.sigmoid(gate) * up).astype(BF16)
        y = y + jnp.dot(act, wd_ref[cols, :], preferred_element_type=F32)
    return x + 0.5 * y


def _ffn_kernel(x_ref, g_ref, wg_ref, wu_ref, wd_ref, o_ref):
    o_ref[...] = _ffn_body(x_ref[...], g_ref, wg_ref, wu_ref, wd_ref)


def _ffn_ple_kernel(x_ref, g_ref, wg_ref, wu_ref, wd_ref, p_ref, pg_ref, wpg_ref, wpp_ref, o_ref):
    h = _ffn_body(x_ref[...], g_ref, wg_ref, wu_ref, wd_ref)
    hn = (h * _rms_scale(h) * pg_ref[...]).astype(BF16)
    gate = jax.nn.sigmoid(jnp.dot(hn, wpg_ref[...], preferred_element_type=F32))
    emb = jnp.dot(p_ref[...].astype(BF16), wpp_ref[...], preferred_element_type=F32)
    o_ref[...] = h + gate * emb


def _ffn_call(h, g, wg, wu, wd, ple=None):
    tok = lambda w: pl.BlockSpec((TM, w), lambda i: (i, 0))
    in_specs = [tok(D_MODEL), _const_spec((1, D_MODEL)), _const_spec((D_MODEL, D_FF_PAD)),
                _const_spec((D_MODEL, D_FF_PAD)), _const_spec((D_FF_PAD, D_MODEL))]
    args = [h, g, wg, wu, wd]
    body = _ffn_kernel
    if ple is not None:
        p, pg, wpg, wpp = ple
        in_specs += [tok(PLE_DIM), _const_spec((1, D_MODEL)), _const_spec((D_MODEL, D_MODEL)),
                     _const_spec((PLE_DIM, D_MODEL))]
        args += [p, pg, wpg, wpp]
        body = _ffn_ple_kernel
    return pl.pallas_call(
        body,
        grid=(N_TOK // TM,),
        in_specs=in_specs,
        out_specs=tok(D_MODEL),
        out_shape=jax.ShapeDtypeStruct((N_TOK, D_MODEL), F32),
        compiler_params=pltpu.CompilerParams(dimension_semantics=("arbitrary",),
                                             vmem_limit_bytes=VMEM_LIMIT),
        name="ffn_ple" if ple is not None else "ffn",
    )(*args)


CONV_COLS = 512


def _inproj_kernel(h_ref, g_ref, wqk_ref, wv_ref, wo_ref, wif_ref, wift_ref, waq_ref, wak_ref, wav_ref,
                   wgm_ref, wga_ref, cw_ref, cb_ref, gbc_ref, gbr_ref, pq_ref, pk_ref, qg_ref, kg_ref,
                   mq_ref, mk_ref, mv_ref, so_ref, gcol_ref, grow_ref, aq_ref, ak_ref, av_ref,
                   sgm_ref, sga_ref, tail_ref):
    i = pl.program_id(0)

    @pl.when(i % (SEQ // TM) == 0)
    def _():
        tail_ref[...] = jnp.zeros(tail_ref.shape, F32)

    x = h_ref[...]
    u = (x * _rms_scale(x) * g_ref[...]).astype(BF16)

    row8 = lax.broadcasted_iota(jnp.int32, (SUBLANES, CONV_COLS), 0)
    for c in range(2 * M_WIDTH // CONV_COLS):
        cols = slice(c * CONV_COLS, (c + 1) * CONV_COLS)
        z = jnp.dot(u, wqk_ref[:, cols], preferred_element_type=F32)
        z8 = z[0:SUBLANES]
        tail = tail_ref[:, cols]
        acc = z * cw_ref[CONV_K - 1:CONV_K, cols] + cb_ref[:, cols]
        acc8 = z8 * cw_ref[CONV_K - 1:CONV_K, cols] + cb_ref[:, cols]
        for s in range(1, CONV_K):
            w = cw_ref[CONV_K - 1 - s:CONV_K - s, cols]
            acc = acc + pltpu.roll(z, s, 0) * w
            acc8 = acc8 + jnp.where(row8 >= s, pltpu.roll(z8, s, 0), pltpu.roll(tail, s, 0)) * w
        tail_ref[:, cols] = z[TM - SUBLANES:TM]
        acc = jnp.concatenate([acc8, acc[SUBLANES:]], axis=0)
        y = acc * jax.nn.sigmoid(acc)
        if c < M_WIDTH // CONV_COLS:
            qcols = cols
            mq_ref[:, qcols] = (y * (M_HEAD_DIM ** -0.5)).astype(BF16)
        else:
            kcols = slice(c * CONV_COLS - M_WIDTH, (c + 1) * CONV_COLS - M_WIDTH)
            mk_ref[:, kcols] = y.astype(BF16)

    mv_ref[...] = jnp.dot(u, wv_ref[...], preferred_element_type=F32).astype(BF16)
    so_ref[...] = jax.nn.sigmoid(jnp.dot(u, wo_ref[...], preferred_element_type=F32))
    sgm_ref[...] = jax.nn.sigmoid(jnp.dot(u, wgm_ref[...], preferred_element_type=F32))
    sga_ref[...] = jax.nn.sigmoid(jnp.dot(u, wga_ref[...], preferred_element_type=F32))

    zc = jnp.dot(u, wif_ref[...], preferred_element_type=F32) + gbc_ref[...]
    lane = lax.broadcasted_iota(jnp.int32, zc.shape, 1)
    is_f = (lane >= M_HEADS) & (lane < 2 * M_HEADS)
    vc = jnp.where(is_f, _log_sigmoid(zc), zc)
    gcol_ref[...] = jnp.where(is_f, _segment_cumsum(vc, 0, M_CHUNK), vc)
    zr = lax.dot_general(wift_ref[...], u, (((1,), (1,)), ((), ())), preferred_element_type=F32) + gbr_ref[...]
    srow = lax.broadcasted_iota(jnp.int32, zr.shape, 0)
    is_fr = (srow >= M_HEADS) & (srow < 2 * M_HEADS)
    vr = jnp.where(is_fr, _log_sigmoid(zr), zr)
    grow_ref[...] = jnp.where(is_fr, _segment_cumsum(vr, 1, M_CHUNK), vr)

    zq = jnp.dot(u, waq_ref[...], preferred_element_type=F32)
    ssq = jnp.dot((zq * zq).astype(BF16), pq_ref[...], preferred_element_type=F32)
    aq_ref[...] = (zq * lax.rsqrt(ssq * (1.0 / A_HEAD_DIM) + EPS) * qg_ref[...]).astype(BF16)
    zk = jnp.dot(u, wak_ref[...], preferred_element_type=F32)
    ssk = jnp.dot((zk * zk).astype(BF16), pk_ref[...], preferred_element_type=F32)
    ak_ref[...] = (zk * lax.rsqrt(ssk * (1.0 / A_HEAD_DIM) + EPS) * kg_ref[...]).astype(BF16)
    av_ref[...] = jnp.dot(u, wav_ref[...], preferred_element_type=F32).astype(BF16)


def _inproj_call(h, w):
    tok = lambda width: pl.BlockSpec((TM, width), lambda i: (i, 0))
    consts = [w["mix_norm"], w["wqk"], w["wv"], w["wo"], w["wif"], w["wift"], w["waq"], w["wak"], w["wav"],
              w["wgm"], w["wga"], w["conv_w"], w["conv_b"], w["gate_b_col"], w["gate_b_row"],
              w["pq"], w["pk"], w["q_gain"], w["k_gain"]]
    out_shape = [
        jax.ShapeDtypeStruct((N_TOK, M_WIDTH), BF16),
        jax.ShapeDtypeStruct((N_TOK, M_WIDTH), BF16),
        jax.ShapeDtypeStruct((N_TOK, M_WIDTH), BF16),
        jax.ShapeDtypeStruct((N_TOK, M_WIDTH), F32),
        jax.ShapeDtypeStruct((N_TOK, LANES), F32),
        jax.ShapeDtypeStruct((GATE_ROWS, N_TOK), F32),
        jax.ShapeDtypeStruct((N_TOK, A_WIDTH), BF16),
        jax.ShapeDtypeStruct((N_TOK, A_KV_WIDTH), BF16),
        jax.ShapeDtypeStruct((N_TOK, A_KV_WIDTH), BF16),
        jax.ShapeDtypeStruct((N_TOK, D_MODEL), F32),
        jax.ShapeDtypeStruct((N_TOK, D_MODEL), F32),
    ]
    out_specs = [tok(M_WIDTH), tok(M_WIDTH), tok(M_WIDTH), tok(M_WIDTH), tok(LANES),
                 pl.BlockSpec((GATE_ROWS, TM), lambda i: (0, i)),
                 tok(A_WIDTH), tok(A_KV_WIDTH), tok(A_KV_WIDTH), tok(D_MODEL), tok(D_MODEL)]
    return pl.pallas_call(
        _inproj_kernel,
        grid=(N_TOK // TM,),
        in_specs=[tok(D_MODEL)] + [_const_spec(c.shape) for c in consts],
        out_specs=out_specs,
        out_shape=out_shape,
        scratch_shapes=[pltpu.VMEM((SUBLANES, 2 * M_WIDTH), F32)],
        compiler_params=pltpu.CompilerParams(dimension_semantics=("arbitrary",),
                                             vmem_limit_bytes=VMEM_LIMIT),
        name="inproj",
    )(h, *consts)


def _mlstm_kernel(q_ref, k_ref, v_ref, so_ref, gcol_ref, grow_ref, og_ref, o_ref, c_ref, n_ref, m_ref):
    L = M_CHUNK

    @pl.when(pl.program_id(1) == 0)
    def _():
        c_ref[...] = jnp.zeros(c_ref.shape, F32)
        n_ref[...] = jnp.zeros(n_ref.shape, F32)
        m_ref[...] = jnp.zeros(m_ref.shape, F32)

    gcol = gcol_ref[...]
    grow = grow_ref[...]
    t_idx = lax.broadcasted_iota(jnp.int32, (L, L), 0)
    s_idx = lax.broadcasted_iota(jnp.int32, (L, L), 1)
    causal = t_idx >= s_idx
    for hd in range(M_HEADS):
        cols = slice(hd * M_HEAD_DIM, (hd + 1) * M_HEAD_DIM)
        q = q_ref[:, cols]
        k = k_ref[:, cols]
        v = v_ref[:, cols]
        a_row = grow[hd:hd + 1, :] - grow[M_HEADS + hd:M_HEADS + hd + 1, :]
        a_col = gcol[:, hd:hd + 1] - gcol[:, M_HEADS + hd:M_HEADS + hd + 1]
        b_col = gcol[:, M_HEADS + hd:M_HEADS + hd + 1]
        m_prev = m_ref[hd:hd + 1, 0:1]
        n_prev = n_ref[hd:hd + 1, :]
        ck_prev = c_ref[hd]

        dmat = jnp.where(causal, a_row, -jnp.inf)
        m_loc = jnp.maximum(jnp.max(dmat, axis=-1, keepdims=True), m_prev)
        w_intra = jnp.exp(dmat - m_loc)
        w_inter = jnp.exp(m_prev - m_loc)
        s = lax.dot_general(q, k, (((1,), (1,)), ((), ())), preferred_element_type=F32) * w_intra
        num = (jnp.dot(s.astype(BF16), v, preferred_element_type=F32)
               + w_inter * jnp.dot(q, ck_prev.astype(BF16), preferred_element_type=F32))
        qn = jnp.sum(q.astype(F32) * n_prev, axis=-1, keepdims=True)
        den = jnp.sum(s, axis=-1, keepdims=True) + w_inter * qn
        hh = num / jnp.maximum(jnp.abs(den), jnp.exp(-(b_col + m_loc)))
        hn = hh * _rms_scale(hh) * og_ref[:, cols]
        o_ref[:, cols] = (so_ref[:, cols] * hn).astype(BF16)

        m_last = m_loc[L - 1:L, :]
        kw = k.astype(F32) * jnp.exp(a_col - m_last)
        decay = jnp.exp(m_prev - m_last)
        c_ref[hd] = decay * ck_prev + lax.dot_general(kw.astype(BF16), v, (((0,), (0,)), ((), ())),
                                                      preferred_element_type=F32)
        n_ref[hd:hd + 1, :] = decay * n_prev + jnp.sum(kw, axis=0, keepdims=True)
        m_ref[hd:hd + 1, :] = jnp.broadcast_to(b_col[L - 1:L, :] + m_last, (1, LANES))


def _mlstm_call(mq, mk, mv, so, gcol, grow, out_gain):
    nc = SEQ // M_CHUNK
    tok = lambda width: pl.BlockSpec((M_CHUNK, width), lambda b, c: (b * nc + c, 0))
    return pl.pallas_call(
        _mlstm_kernel,
        grid=(BATCH, nc),
        in_specs=[tok(M_WIDTH), tok(M_WIDTH), tok(M_WIDTH), tok(M_WIDTH), tok(LANES),
                  pl.BlockSpec((GATE_ROWS, M_CHUNK), lambda b, c: (0, b * nc + c)),
                  _const_spec((1, M_WIDTH))],
        out_specs=tok(M_WIDTH),
        out_shape=jax.ShapeDtypeStruct((N_TOK, M_WIDTH), BF16),
        scratch_shapes=[pltpu.VMEM((M_HEADS, M_HEAD_DIM, M_HEAD_DIM), F32),
                        pltpu.VMEM((SUBLANES, M_HEAD_DIM), F32),
                        pltpu.VMEM((SUBLANES, LANES), F32)],
        compiler_params=pltpu.CompilerParams(dimension_semantics=("arbitrary", "arbitrary"),
                                             vmem_limit_bytes=VMEM_LIMIT),
        name="mlstm",
    )(mq, mk, mv, so, gcol, grow, out_gain)


def _swa_kernel(q_ref, kp_ref, kc_ref, vp_ref, vc_ref, sink_ref, o_ref):
    n = pl.program_id(1)
    blk = A_BLOCK
    qi = lax.broadcasted_iota(jnp.int32, (A_GROUP * blk, 2 * blk), 0) % blk
    kj = lax.broadcasted_iota(jnp.int32, (A_GROUP * blk, 2 * blk), 1)
    rel = blk + qi - kj
    valid = (rel >= 0) & (rel < WINDOW) & ((kj >= blk) | (n > 0))
    q = q_ref[...]
    outs = []
    for hk in range(A_KV_HEADS):
        kcols = slice(hk * A_HEAD_DIM, (hk + 1) * A_HEAD_DIM)
        k = jnp.concatenate([kp_ref[:, kcols], kc_ref[:, kcols]], axis=0)
        v = jnp.concatenate([vp_ref[:, kcols], vc_ref[:, kcols]], axis=0)
        heads = range(hk * A_GROUP, (hk + 1) * A_GROUP)
        qg = jnp.concatenate([q[:, hq * A_HEAD_DIM:(hq + 1) * A_HEAD_DIM] for hq in heads], axis=0)
        sink = jnp.concatenate([jnp.broadcast_to(sink_ref[hq:hq + 1, 0:1], (blk, 1)) for hq in heads], axis=0)
        s = lax.dot_general(qg, k, (((1,), (1,)), ((), ())), preferred_element_type=F32)
        s = jnp.where(valid, s, -jnp.inf)
        mx = jnp.maximum(jnp.max(s, axis=-1, keepdims=True), sink)
        p = jnp.exp(s - mx)
        den = jnp.sum(p, axis=-1, keepdims=True) + jnp.exp(sink - mx)
        o = jnp.dot(p.astype(BF16), v, preferred_element_type=F32) / den
        outs += [o[g * blk:(g + 1) * blk] for g in range(A_GROUP)]
    o_ref[...] = jnp.concatenate(outs, axis=1).astype(BF16)


def _swa_call(aq, ak, av, sinks):
    nb = SEQ // A_BLOCK
    cur = lambda width: pl.BlockSpec((A_BLOCK, width), lambda b, n: (b * nb + n, 0))
    prev = lambda width: pl.BlockSpec((A_BLOCK, width), lambda b, n: (b * nb + jnp.maximum(n - 1, 0), 0))
    return pl.pallas_call(
        _swa_kernel,
        grid=(BATCH, nb),
        in_specs=[cur(A_WIDTH), prev(A_KV_WIDTH), cur(A_KV_WIDTH), prev(A_KV_WIDTH), cur(A_KV_WIDTH),
                  _const_spec((A_Q_HEADS, LANES))],
        out_specs=cur(A_WIDTH),
        out_shape=jax.ShapeDtypeStruct((N_TOK, A_WIDTH), BF16),
        compiler_params=pltpu.CompilerParams(dimension_semantics=("arbitrary", "arbitrary"),
                                             vmem_limit_bytes=VMEM_LIMIT),
        name="swa",
    )(aq, ak, ak, av, av, sinks)


def _merge_kernel(h_ref, hm_ref, ha_ref, sgm_ref, sga_ref, wm_ref, wa_ref, wo_ref, o_ref):
    bm = jnp.dot(hm_ref[...], wm_ref[...], preferred_element_type=F32)
    ba = jnp.dot(ha_ref[...], wa_ref[...], preferred_element_type=F32)
    merged = (sgm_ref[...] * bm + sga_ref[...] * ba).astype(BF16)
    o_ref[...] = h_ref[...] + jnp.dot(merged, wo_ref[...], preferred_element_type=F32)


def _merge_call(h, hm, ha, sgm, sga, wm, wa, wo):
    tok = pl.BlockSpec((TM, D_MODEL), lambda i: (i, 0))
    wspec = _const_spec((D_MODEL, D_MODEL))
    return pl.pallas_call(
        _merge_kernel,
        grid=(N_TOK // TM,),
        in_specs=[tok, tok, tok, tok, tok, wspec, wspec, wspec],
        out_specs=tok,
        out_shape=jax.ShapeDtypeStruct((N_TOK, D_MODEL), F32),
        compiler_params=pltpu.CompilerParams(dimension_semantics=("arbitrary",),
                                             vmem_limit_bytes=VMEM_LIMIT),
        name="merge",
    )(h, hm, ha, sgm, sga, wm, wa, wo)


def _block_ones(width, block):
    idx = np.arange(width) // block
    return jnp.asarray(idx[:, None] == idx[None, :], dtype=BF16)


def _prep_layer(i, ffn1_norm, ffn1_w_gate, ffn1_w_up, ffn1_w_down, mix_norm, w_in, m_conv_w, m_conv_b,
                m_igate_b, m_fgate_b, m_out_norm, a_q_norm, a_k_norm, a_sinks, w_branch_m, w_branch_a,
                w_out, ffn2_norm, ffn2_w_gate, ffn2_w_up, ffn2_w_down, ple_norm, ple_gate_w, ple_proj_w):
    ff_pad = D_FF_PAD - D_FF
    row = lambda v: v[i].reshape(1, -1).astype(F32)
    wide = lambda wt: jnp.pad(wt[i], ((0, 0), (0, ff_pad))).astype(BF16)
    tall = lambda wt: jnp.pad(wt[i], ((0, ff_pad), (0, 0))).astype(BF16)
    sizes = [2 * M_WIDTH, M_WIDTH, M_WIDTH, 2 * M_HEADS, A_WIDTH, A_KV_WIDTH, A_KV_WIDTH, D_MODEL, D_MODEL]
    offs = np.concatenate([[0], np.cumsum(sizes)])
    part = lambda j: w_in[i][:, int(offs[j]):int(offs[j + 1])].astype(BF16)
    wif = part(3)
    gate_b = jnp.concatenate([m_igate_b[i], m_fgate_b[i]]).astype(F32)
    return dict(
        ffn1=(row(ffn1_norm), wide(ffn1_w_gate), wide(ffn1_w_up), tall(ffn1_w_down)),
        ffn2=(row(ffn2_norm), wide(ffn2_w_gate), wide(ffn2_w_up), tall(ffn2_w_down)),
        ple=(row(ple_norm), ple_gate_w[i].astype(BF16), ple_proj_w[i].astype(BF16)),
        mix_norm=row(mix_norm), wqk=part(0), wv=part(1), wo=part(2),
        wif=jnp.pad(wif, ((0, 0), (0, LANES - 2 * M_HEADS))),
        wift=jnp.pad(wif.T, ((0, GATE_ROWS - 2 * M_HEADS), (0, 0))),
        waq=part(4), wak=part(5), wav=part(6), wgm=part(7), wga=part(8),
        conv_w=m_conv_w[i].astype(F32), conv_b=row(m_conv_b),
        gate_b_col=jnp.pad(gate_b, (0, LANES - 2 * M_HEADS)).reshape(1, LANES),
        gate_b_row=jnp.pad(gate_b, (0, GATE_ROWS - 2 * M_HEADS)).reshape(GATE_ROWS, 1),
        pq=_block_ones(A_WIDTH, A_HEAD_DIM), pk=_block_ones(A_KV_WIDTH, A_HEAD_DIM),
        q_gain=(jnp.tile(a_q_norm[i], A_Q_HEADS) * (A_HEAD_DIM ** -0.5)).reshape(1, A_WIDTH).astype(F32),
        k_gain=jnp.tile(a_k_norm[i], A_KV_HEADS).reshape(1, A_KV_WIDTH).astype(F32),
        out_gain=row(m_out_norm),
        sinks=jnp.broadcast_to(a_sinks[i].astype(F32)[:, None], (A_Q_HEADS, LANES)),
        wm=w_branch_m[i].astype(BF16), wa=w_branch_a[i].astype(BF16), wout=w_out[i].astype(BF16),
    )


def kernel(x, p, ffn1_norm, ffn1_w_gate, ffn1_w_up, ffn1_w_down, mix_norm, w_in, m_conv_w, m_conv_b, m_igate_b, m_fgate_b, m_out_norm, a_q_norm, a_k_norm, a_sinks, w_branch_m, w_branch_a, w_out, ffn2_norm, ffn2_w_gate, ffn2_w_up, ffn2_w_down, ple_norm, ple_gate_w, ple_proj_w):
    h = x.reshape(N_TOK, D_MODEL)
    for i in range(DEPTH):
        w = _prep_layer(i, ffn1_norm, ffn1_w_gate, ffn1_w_up, ffn1_w_down, mix_norm, w_in, m_conv_w,
                        m_conv_b, m_igate_b, m_fgate_b, m_out_norm, a_q_norm, a_k_norm, a_sinks,
                        w_branch_m, w_branch_a, w_out, ffn2_norm, ffn2_w_gate, ffn2_w_up, ffn2_w_down,
                        ple_norm, ple_gate_w, ple_proj_w)
        h = _ffn_call(h, *w["ffn1"])
        mq, mk, mv, so, gcol, grow, aq, ak, av, sgm, sga = _inproj_call(h, w)
        hm = _mlstm_call(mq, mk, mv, so, gcol, grow, w["out_gain"])
        ha = _swa_call(aq, ak, av, w["sinks"])
        h = _merge_call(h, hm, ha, sgm, sga, w["wm"], w["wa"], w["wout"])
        pg, wpg, wpp = w["ple"]
        h = _ffn_call(h, *w["ffn2"], ple=(p[i].reshape(N_TOK, PLE_DIM), pg, wpg, wpp))
    return h.reshape(BATCH, SEQ, D_MODEL)
```

```python
import math

import jax
import jax.numpy as jnp
import numpy as np
from jax import lax
from jax.experimental import pallas as pl
from jax.experimental.pallas import tpu as pltpu

D_MODEL = 1024
BATCH = 8
SEQ = 4096
DEPTH = 2
PLE_DIM = 256
D_FF = 2752
M_HEADS = 4
M_HEAD_DIM = 256
M_WIDTH = M_HEADS * M_HEAD_DIM
CONV_K = 4
A_Q_HEADS = 16
A_KV_HEADS = 4
A_HEAD_DIM = 64
A_GROUP = A_Q_HEADS // A_KV_HEADS
A_WIDTH = A_Q_HEADS * A_HEAD_DIM
A_KV_WIDTH = A_KV_HEADS * A_HEAD_DIM
WINDOW = 128
A_BLOCK = 128
EPS = 1e-6

N_TOK = BATCH * SEQ
LANES = 128
SUBLANES = 8
MXU_DIM = 256
D_FF_PAD = -(-D_FF // MXU_DIM) * MXU_DIM
TM = 512
M_CHUNK = 256
M_ROWS = 128
A_KV_DUP = A_KV_HEADS * LANES
VMEM_LIMIT = 56 * 1024 * 1024
LOG2E = math.log2(math.e)

BF16 = jnp.bfloat16
F32 = jnp.float32


def _const_spec(shape):
    zeros = (0,) * len(shape)
    return pl.BlockSpec(shape, lambda *_: zeros, pipeline_mode=pl.Buffered(1))


def _rms_scale(x):
    return lax.rsqrt(jnp.mean(x * x, axis=-1, keepdims=True) + EPS)


def _log_sigmoid(x):
    return jnp.minimum(x, 0.0) - jnp.log1p(jnp.exp(-jnp.abs(x)))


def _segment_scan(x, axis, seg, op, fill):
    pos = lax.broadcasted_iota(jnp.int32, x.shape, axis) % seg
    shift = 1
    while shift < seg:
        x = op(x, jnp.where(pos >= shift, pltpu.roll(x, shift, axis), fill))
        shift *= 2
    return x


def _ffn_body(x, g_ref, wg_ref, wu_ref, wd_ref):
    xn = (x * _rms_scale(x) * g_ref[...]).astype(BF16)
    y = jnp.zeros(x.shape, F32)
    for c in range(D_FF_PAD // MXU_DIM):
        cols = slice(c * MXU_DIM, (c + 1) * MXU_DIM)
        gate = jnp.dot(xn, wg_ref[:, cols], preferred_element_type=F32)
        up = jnp.dot(xn, wu_ref[:, cols], preferred_element_type=F32)
        act = (gate * jax.nn.sigmoid(gate) * up).astype(BF16)
        y = y + jnp.dot(act, wd_ref[cols, :], preferred_element_type=F32)
    return x + 0.5 * y


def _ffn_kernel(x_ref, g_ref, wg_ref, wu_ref, wd_ref, o_ref):
    o_ref[...] = _ffn_body(x_ref[...], g_ref, wg_ref, wu_ref, wd_ref)


def _ffn_ple_kernel(x_ref, g_ref, wg_ref, wu_ref, wd_ref, p_ref, pg_ref, wpg_ref, wpp_ref, o_ref):
    h = _ffn_body(x_ref[...], g_ref, wg_ref, wu_ref, wd_ref)
    hn = (h * _rms_scale(h) * pg_ref[...]).astype(BF16)
    gate = jax.nn.sigmoid(jnp.dot(hn, wpg_ref[...], preferred_element_type=F32))
    emb = jnp.dot(p_ref[...].astype(BF16), wpp_ref[...], preferred_element_type=F32)
    o_ref[...] = h + gate * emb


def _ffn_call(h, g, wg, wu, wd, ple=None):
    tok = lambda w: pl.BlockSpec((TM, w), lambda i: (i, 0))
    in_specs = [tok(D_MODEL), _const_spec((1, D_MODEL)), _const_spec((D_MODEL, D_FF_PAD)),
                _const_spec((D_MODEL, D_FF_PAD)), _const_spec((D_FF_PAD, D_MODEL))]
    args = [h, g, wg, wu, wd]
    body = _ffn_kernel
    if ple is not None:
        p, pg, wpg, wpp = ple
        in_specs += [tok(PLE_DIM), _const_spec((1, D_MODEL)), _const_spec((D_MODEL, D_MODEL)),
                     _const_spec((PLE_DIM, D_MODEL))]
        args += [p, pg, wpg, wpp]
        body = _ffn_ple_kernel
    return pl.pallas_call(
        body,
        grid=(N_TOK // TM,),
        in_specs=in_specs,
        out_specs=tok(D_MODEL),
        out_shape=jax.ShapeDtypeStruct((N_TOK, D_MODEL), F32),
        compiler_params=pltpu.CompilerParams(dimension_semantics=("arbitrary",),
                                             vmem_limit_bytes=VMEM_LIMIT),
        name="ffn_ple" if ple is not None else "ffn",
    )(*args)


CONV_COLS = 256
CONV_ROWS = 64
PLAIN_PARTS = (2 * M_WIDTH // CONV_COLS) // 4
PLAIN_COLS = D_MODEL // PLAIN_PARTS


def _inproj_kernel(h_ref, g_ref, wqk_ref, wv_ref, wo_ref, wif_ref, wift_ref, waq_ref, wak_ref, wav_ref,
                   wgm_ref, wga_ref, cw_ref, cb_ref, gbc_ref, gbr_ref, pq_ref, pk_ref, qg_ref, kg_ref, og_ref,
                   mq_ref, mk_ref, mv_ref, sog_ref, ga_ref, gb_ref, gm_ref, grow_ref, aq_ref, ak_ref, av_ref,
                   sgm_ref, sga_ref, zbuf_ref):
    i = pl.program_id(0)

    @pl.when(i % (SEQ // TM) == 0)
    def _():
        zbuf_ref[0:SUBLANES, :] = jnp.zeros((SUBLANES, 2 * M_WIDTH), F32)

    x = h_ref[...]
    u = (x * _rms_scale(x) * g_ref[...]).astype(BF16)

    for c in range(2 * M_WIDTH // CONV_COLS):
        cols = slice(c * CONV_COLS, (c + 1) * CONV_COLS)
        zbuf_ref[SUBLANES:, cols] = jnp.dot(u, wqk_ref[:, cols], preferred_element_type=F32)
        for rb in range(TM // CONV_ROWS):
            acc = cb_ref[:, cols]
            for s in range(CONV_K):
                tap = zbuf_ref[pl.ds(SUBLANES + rb * CONV_ROWS - s, CONV_ROWS), cols]
                acc = acc + tap * cw_ref[CONV_K - 1 - s:CONV_K - s, cols]
            y = acc * jax.nn.sigmoid(acc)
            rows = slice(rb * CONV_ROWS, (rb + 1) * CONV_ROWS)
            if c < M_WIDTH // CONV_COLS:
                mq_ref[rows, cols] = (y * (M_HEAD_DIM ** -0.5)).astype(BF16)
            else:
                mk_ref[rows, slice(cols.start - M_WIDTH, cols.stop - M_WIDTH)] = y.astype(BF16)
        zbuf_ref[0:SUBLANES, cols] = zbuf_ref[TM:TM + SUBLANES, cols]
        which, part = divmod(c, PLAIN_PARTS)
        pc = slice(part * PLAIN_COLS, (part + 1) * PLAIN_COLS)
        if which == 0:
            mv_ref[:, pc] = jnp.dot(u, wv_ref[:, pc], preferred_element_type=F32).astype(BF16)
        elif which == 1:
            sog_ref[:, pc] = jax.nn.sigmoid(jnp.dot(u, wo_ref[:, pc], preferred_element_type=F32)) * og_ref[:, pc]
        elif which == 2:
            sgm_ref[:, pc] = jax.nn.sigmoid(jnp.dot(u, wgm_ref[:, pc], preferred_element_type=F32))
        else:
            sga_ref[:, pc] = jax.nn.sigmoid(jnp.dot(u, wga_ref[:, pc], preferred_element_type=F32))

    zc = jnp.dot(u, wif_ref[...], preferred_element_type=F32) + gbc_ref[...]
    b_col = _segment_scan(_log_sigmoid(zc[:, LANES:]) * LOG2E, 0, M_CHUNK, jnp.add, 0.0)
    a_col = zc[:, :LANES] * LOG2E - b_col
    ga_ref[...] = a_col
    gb_ref[...] = b_col
    gm_ref[...] = _segment_scan(a_col, 0, M_CHUNK, jnp.maximum, -jnp.inf)
    zr = lax.dot_general(wift_ref[...], u, (((1,), (1,)), ((), ())), preferred_element_type=F32) + gbr_ref[...]
    b_row = _segment_scan(_log_sigmoid(zr[SUBLANES:]) * LOG2E, 1, M_CHUNK, jnp.add, 0.0)
    grow_ref[...] = zr[:SUBLANES] * LOG2E - b_row

    zq = jnp.dot(u, waq_ref[...], preferred_element_type=F32)
    ssq = jnp.dot((zq * zq).astype(BF16), pq_ref[...], preferred_element_type=F32)
    aq_ref[...] = (zq * lax.rsqrt(ssq * (1.0 / A_HEAD_DIM) + EPS) * qg_ref[...]).astype(BF16)
    zk = jnp.dot(u, wak_ref[...], preferred_element_type=F32)
    ssk = jnp.dot((zk * zk).astype(BF16), pk_ref[...], preferred_element_type=F32)
    ak_ref[...] = (zk * lax.rsqrt(ssk * (1.0 / A_HEAD_DIM) + EPS) * kg_ref[...]).astype(BF16)
    av_ref[...] = jnp.dot(u, wav_ref[...], preferred_element_type=F32).astype(BF16)


def _inproj_call(h, w):
    tok = lambda width: pl.BlockSpec((TM, width), lambda i: (i, 0))
    consts = [w["mix_norm"], w["wqk"], w["wv"], w["wo"], w["wif"], w["wift"], w["waq"], w["wak"], w["wav"],
              w["wgm"], w["wga"], w["conv_w"], w["conv_b"], w["gate_b_col"], w["gate_b_row"],
              w["pq"], w["pk"], w["q_gain"], w["k_gain"], w["out_gain"]]
    out_shape = [
        jax.ShapeDtypeStruct((N_TOK, M_WIDTH), BF16),
        jax.ShapeDtypeStruct((N_TOK, M_WIDTH), BF16),
        jax.ShapeDtypeStruct((N_TOK, M_WIDTH), BF16),
        jax.ShapeDtypeStruct((N_TOK, M_WIDTH), F32),
        jax.ShapeDtypeStruct((N_TOK, LANES), F32),
        jax.ShapeDtypeStruct((N_TOK, LANES), F32),
        jax.ShapeDtypeStruct((N_TOK, LANES), F32),
        jax.ShapeDtypeStruct((SUBLANES, N_TOK), F32),
        jax.ShapeDtypeStruct((N_TOK, A_WIDTH), BF16),
        jax.ShapeDtypeStruct((N_TOK, A_KV_DUP), BF16),
        jax.ShapeDtypeStruct((N_TOK, A_KV_DUP), BF16),
        jax.ShapeDtypeStruct((N_TOK, D_MODEL), F32),
        jax.ShapeDtypeStruct((N_TOK, D_MODEL), F32),
    ]
    out_specs = [tok(M_WIDTH), tok(M_WIDTH), tok(M_WIDTH), tok(M_WIDTH), tok(LANES), tok(LANES), tok(LANES),
                 pl.BlockSpec((SUBLANES, TM), lambda i: (0, i)),
                 tok(A_WIDTH), tok(A_KV_DUP), tok(A_KV_DUP), tok(D_MODEL), tok(D_MODEL)]
    return pl.pallas_call(
        _inproj_kernel,
        grid=(N_TOK // TM,),
        in_specs=[tok(D_MODEL)] + [_const_spec(c.shape) for c in consts],
        out_specs=out_specs,
        out_shape=out_shape,
        scratch_shapes=[pltpu.VMEM((SUBLANES + TM, 2 * M_WIDTH), F32)],
        compiler_params=pltpu.CompilerParams(dimension_semantics=("arbitrary",),
                                             vmem_limit_bytes=VMEM_LIMIT),
        name="inproj",
    )(h, *consts)


def _mlstm_kernel(q_ref, k_ref, v_ref, sog_ref, ga_ref, gb_ref, gm_ref, grow_ref, o_ref, c_ref, n_ref, m_ref):
    L, R = M_CHUNK, M_ROWS

    @pl.when(pl.program_id(1) == 0)
    def _():
        c_ref[...] = jnp.zeros(c_ref.shape, F32)
        n_ref[...] = jnp.zeros(n_ref.shape, F32)
        m_ref[...] = jnp.zeros(m_ref.shape, F32)

    tri = lax.broadcasted_iota(jnp.int32, (R, R), 0) >= lax.broadcasted_iota(jnp.int32, (R, R), 1)
    lane = lax.broadcasted_iota(jnp.int32, (L, LANES), 1)
    ga, gb, gm = ga_ref[...], gb_ref[...], gm_ref[...]
    for hd in range(M_HEADS):
        cols = slice(hd * M_HEAD_DIM, (hd + 1) * M_HEAD_DIM)
        a_row = grow_ref[hd:hd + 1, :]
        pick = lane == hd
        a_col = jnp.sum(jnp.where(pick, ga, 0.0), axis=-1, keepdims=True)
        b_col = jnp.sum(jnp.where(pick, gb, 0.0), axis=-1, keepdims=True)
        cm_col = jnp.sum(jnp.where(pick, gm, 0.0), axis=-1, keepdims=True)
        m_prev = jnp.max(m_ref[hd:hd + 1, :], axis=-1, keepdims=True)
        n_prev = n_ref[hd:hd + 1, :]
        ck_prev = c_ref[hd]
        ck_bf = ck_prev.astype(BF16)
        m_loc = jnp.maximum(cm_col, m_prev)
        w_inter = jnp.exp2(m_prev - m_loc)
        floor = jnp.exp2(-(b_col + m_loc))
        k = k_ref[:, cols]
        v = v_ref[:, cols]
        for rb in range(L // R):
            rows = slice(rb * R, (rb + 1) * R)
            width = (rb + 1) * R
            q = q_ref[rows, cols]
            s = lax.dot_general(q, k[:width], (((1,), (1,)), ((), ())), preferred_element_type=F32)
            ml = m_loc[rows]
            wi = w_inter[rows]
            blocks = []
            for cb in range(rb + 1):
                e = jnp.exp2(a_row[:, cb * R:(cb + 1) * R] - ml)
                if cb == rb:
                    e = jnp.where(tri, e, 0.0)
                blocks.append(s[:, cb * R:(cb + 1) * R] * e)
            sw = blocks[0] if rb == 0 else jnp.concatenate(blocks, axis=1)
            num = (jnp.dot(sw.astype(BF16), v[:width], preferred_element_type=F32)
                   + wi * jnp.dot(q, ck_bf, preferred_element_type=F32))
            qn = q.astype(F32) * n_prev
            dsum = wi * (qn[:, :LANES] + qn[:, LANES:])
            for blk in blocks:
                dsum = dsum + blk
            den = jnp.sum(dsum, axis=-1, keepdims=True)
            dd = jnp.maximum(jnp.abs(den), floor[rows])
            scale = lax.rsqrt(jnp.mean(num * num, axis=-1, keepdims=True) + EPS * (dd * dd))
            o_ref[rows, cols] = (num * scale * sog_ref[rows, cols]).astype(BF16)

        m_last = m_loc[L - 1:L, :]
        kw = k.astype(F32) * jnp.exp2(a_col - m_last)
        decay = jnp.exp2(m_prev - m_last)
        c_ref[hd] = decay * ck_prev + lax.dot_general(kw.astype(BF16), v, (((0,), (0,)), ((), ())),
                                                      preferred_element_type=F32)
        n_ref[hd:hd + 1, :] = decay * n_prev + jnp.sum(kw, axis=0, keepdims=True)
        m_ref[hd:hd + 1, :] = jnp.broadcast_to(b_col[L - 1:L, :] + m_last, (1, LANES))


def _mlstm_call(mq, mk, mv, sog, ga, gb, gm, grow):
    nc = SEQ // M_CHUNK
    tok = lambda width: pl.BlockSpec((M_CHUNK, width), lambda b, c: (b * nc + c, 0))
    return pl.pallas_call(
        _mlstm_kernel,
        grid=(BATCH, nc),
        in_specs=[tok(M_WIDTH), tok(M_WIDTH), tok(M_WIDTH), tok(M_WIDTH), tok(LANES), tok(LANES), tok(LANES),
                  pl.BlockSpec((SUBLANES, M_CHUNK), lambda b, c: (0, b * nc + c))],
        out_specs=tok(M_WIDTH),
        out_shape=jax.ShapeDtypeStruct((N_TOK, M_WIDTH), BF16),
        scratch_shapes=[pltpu.VMEM((M_HEADS, M_HEAD_DIM, M_HEAD_DIM), F32),
                        pltpu.VMEM((SUBLANES, M_HEAD_DIM), F32),
                        pltpu.VMEM((SUBLANES, LANES), F32)],
        compiler_params=pltpu.CompilerParams(dimension_semantics=("arbitrary", "arbitrary"),
                                             vmem_limit_bytes=VMEM_LIMIT),
        name="mlstm",
    )(mq, mk, mv, sog, ga, gb, gm, grow)


def _swa_kernel(q_ref, kp_ref, kc_ref, vp_ref, vc_ref, sink_ref, o_ref):
    n = pl.program_id(1)
    blk = A_BLOCK
    tri = lax.broadcasted_iota(jnp.int32, (blk, blk), 0) >= lax.broadcasted_iota(jnp.int32, (blk, blk), 1)
    prev_bias = jnp.where(n > 0, 0.0, -jnp.inf)
    low_kv = lax.broadcasted_iota(jnp.int32, (2 * blk, LANES), 1) < A_HEAD_DIM
    low_out = lax.broadcasted_iota(jnp.int32, (blk, LANES), 1) < A_HEAD_DIM
    ones = jnp.ones((2 * blk, LANES), BF16)
    zero_kv = jnp.zeros((2 * blk, LANES), BF16)
    for hk in range(A_KV_HEADS):
        kvcols = slice(hk * LANES, (hk + 1) * LANES)
        kd = jnp.concatenate([kc_ref[:, kvcols], kp_ref[:, kvcols]], axis=0)
        vd = jnp.concatenate([vc_ref[:, kvcols], vp_ref[:, kvcols]], axis=0)
        k_half = (jnp.where(low_kv, kd, zero_kv), jnp.where(low_kv, zero_kv, kd))
        v_ext = jnp.concatenate([vd, ones], axis=1)
        pairs = [hk * (A_GROUP // 2) + j for j in range(A_GROUP // 2)]
        qcat = jnp.concatenate([q_ref[:, pr * LANES:(pr + 1) * LANES] for pr in pairs], axis=0)
        probs, sink_w = {}, {}
        for half in range(2):
            s2 = lax.dot_general(qcat, k_half[half], (((1,), (1,)), ((), ())), preferred_element_type=F32)
            for j, pr in enumerate(pairs):
                hq = 2 * pr + half
                sj = s2[j * blk:(j + 1) * blk]
                s = jnp.where(tri, sj[:, :blk], sj[:, blk:] + prev_bias)
                sink = sink_ref[hq:hq + 1, 0:1]
                mx = jnp.maximum(jnp.broadcast_to(jnp.max(s, axis=-1, keepdims=True), (blk, blk)), sink)
                p = jnp.exp2(s - mx)
                sink_w[hq] = jnp.exp2(sink - mx)
                probs[hq] = jnp.concatenate([jnp.where(tri, p, 0.0), jnp.where(tri, 0.0, p)], axis=1).astype(BF16)
        heads = sorted(probs)
        o2 = jnp.dot(jnp.concatenate([probs[hq] for hq in heads], axis=0), v_ext,
                     preferred_element_type=F32)
        outs = {}
        for r, hq in enumerate(heads):
            oh = o2[r * blk:(r + 1) * blk]
            outs[hq] = oh[:, :LANES] / (oh[:, LANES:] + sink_w[hq])
        for pr in pairs:
            o_ref[:, pr * LANES:(pr + 1) * LANES] = jnp.where(low_out, outs[2 * pr], outs[2 * pr + 1]).astype(BF16)


def _swa_call(aq, ak, av, sinks):
    nb = SEQ // A_BLOCK
    cur = lambda width: pl.BlockSpec((A_BLOCK, width), lambda b, n: (b * nb + n, 0))
    prev = lambda width: pl.BlockSpec((A_BLOCK, width), lambda b, n: (b * nb + jnp.maximum(n - 1, 0), 0))
    return pl.pallas_call(
        _swa_kernel,
        grid=(BATCH, nb),
        in_specs=[cur(A_WIDTH), prev(A_KV_DUP), cur(A_KV_DUP), prev(A_KV_DUP), cur(A_KV_DUP),
                  _const_spec((A_Q_HEADS, LANES))],
        out_specs=cur(A_WIDTH),
        out_shape=jax.ShapeDtypeStruct((N_TOK, A_WIDTH), BF16),
        compiler_params=pltpu.CompilerParams(dimension_semantics=("arbitrary", "arbitrary"),
                                             vmem_limit_bytes=VMEM_LIMIT),
        name="swa",
    )(aq, ak, ak, av, av, sinks)


def _merge_kernel(h_ref, hm_ref, ha_ref, sgm_ref, sga_ref, wm_ref, wa_ref, wo_ref, o_ref):
    bm = jnp.dot(hm_ref[...], wm_ref[...], preferred_element_type=F32)
    ba = jnp.dot(ha_ref[...], wa_ref[...], preferred_element_type=F32)
    merged = (sgm_ref[...] * bm + sga_ref[...] * ba).astype(BF16)
    o_ref[...] = h_ref[...] + jnp.dot(merged, wo_ref[...], preferred_element_type=F32)


def _merge_call(h, hm, ha, sgm, sga, wm, wa, wo):
    tok = pl.BlockSpec((TM, D_MODEL), lambda i: (i, 0))
    wspec = _const_spec((D_MODEL, D_MODEL))
    return pl.pallas_call(
        _merge_kernel,
        grid=(N_TOK // TM,),
        in_specs=[tok, tok, tok, tok, tok, wspec, wspec, wspec],
        out_specs=tok,
        out_shape=jax.ShapeDtypeStruct((N_TOK, D_MODEL), F32),
        compiler_params=pltpu.CompilerParams(dimension_semantics=("arbitrary",),
                                             vmem_limit_bytes=VMEM_LIMIT),
        name="merge",
    )(h, hm, ha, sgm, sga, wm, wa, wo)


def _head_sum_matrix(width, group, live):
    idx = np.arange(width)
    same = (idx[:, None] // group) == (idx[None, :] // group)
    return jnp.asarray(same & ((idx % group) < live)[:, None], dtype=BF16)


def _dup_heads(x):
    lead = x.shape[:-1]
    x = x.reshape(lead + (A_KV_HEADS, 1, A_HEAD_DIM))
    return jnp.broadcast_to(x, lead + (A_KV_HEADS, 2, A_HEAD_DIM)).reshape(lead + (A_KV_DUP,))


def _prep_layer(i, ffn1_norm, ffn1_w_gate, ffn1_w_up, ffn1_w_down, mix_norm, w_in, m_conv_w, m_conv_b,
                m_igate_b, m_fgate_b, m_out_norm, a_q_norm, a_k_norm, a_sinks, w_branch_m, w_branch_a,
                w_out, ffn2_norm, ffn2_w_gate, ffn2_w_up, ffn2_w_down, ple_norm, ple_gate_w, ple_proj_w):
    ff_pad = D_FF_PAD - D_FF
    row = lambda v: v[i].reshape(1, -1).astype(F32)
    wide = lambda wt: jnp.pad(wt[i], ((0, 0), (0, ff_pad))).astype(BF16)
    tall = lambda wt: jnp.pad(wt[i], ((0, ff_pad), (0, 0))).astype(BF16)
    sizes = [2 * M_WIDTH, M_WIDTH, M_WIDTH, M_HEADS, M_HEADS, A_WIDTH, A_KV_WIDTH, A_KV_WIDTH, D_MODEL, D_MODEL]
    offs = np.concatenate([[0], np.cumsum(sizes)])
    part = lambda j: w_in[i][:, int(offs[j]):int(offs[j + 1])].astype(BF16)
    lane_pad = lambda wt: jnp.pad(wt, ((0, 0), (0, LANES - M_HEADS)))
    sub_pad = lambda wt: jnp.pad(wt, ((0, SUBLANES - M_HEADS), (0, 0)))
    w_i, w_f = part(3), part(4)
    b_i, b_f = m_igate_b[i].astype(F32), m_fgate_b[i].astype(F32)
    return dict(
        ffn1=(row(ffn1_norm), wide(ffn1_w_gate), wide(ffn1_w_up), tall(ffn1_w_down)),
        ffn2=(row(ffn2_norm), wide(ffn2_w_gate), wide(ffn2_w_up), tall(ffn2_w_down)),
        ple=(row(ple_norm), ple_gate_w[i].astype(BF16), ple_proj_w[i].astype(BF16)),
        mix_norm=row(mix_norm), wqk=part(0), wv=part(1), wo=part(2),
        wif=jnp.concatenate([lane_pad(w_i), lane_pad(w_f)], axis=1),
        wift=jnp.concatenate([sub_pad(w_i.T), sub_pad(w_f.T)], axis=0),
        gate_b_col=jnp.concatenate([lane_pad(b_i[None]), lane_pad(b_f[None])], axis=1),
        gate_b_row=jnp.concatenate([sub_pad(b_i[:, None]), sub_pad(b_f[:, None])], axis=0),
        waq=part(5), wak=_dup_heads(part(6)), wav=_dup_heads(part(7)), wgm=part(8), wga=part(9),
        conv_w=m_conv_w[i].astype(F32), conv_b=row(m_conv_b),
        pq=_head_sum_matrix(A_WIDTH, A_HEAD_DIM, A_HEAD_DIM), pk=_head_sum_matrix(A_KV_DUP, LANES, A_HEAD_DIM),
        q_gain=(jnp.tile(a_q_norm[i], A_Q_HEADS) * (A_HEAD_DIM ** -0.5 * LOG2E)).reshape(1, A_WIDTH).astype(F32),
        k_gain=jnp.tile(a_k_norm[i], 2 * A_KV_HEADS).reshape(1, A_KV_DUP).astype(F32),
        out_gain=row(m_out_norm),
        sinks=jnp.broadcast_to((a_sinks[i].astype(F32) * LOG2E)[:, None], (A_Q_HEADS, LANES)),
        wm=w_branch_m[i].astype(BF16), wa=w_branch_a[i].astype(BF16), wout=w_out[i].astype(BF16),
    )


def kernel(x, p, ffn1_norm, ffn1_w_gate, ffn1_w_up, ffn1_w_down, mix_norm, w_in, m_conv_w, m_conv_b, m_igate_b, m_fgate_b, m_out_norm, a_q_norm, a_k_norm, a_sinks, w_branch_m, w_branch_a, w_out, ffn2_norm, ffn2_w_gate, ffn2_w_up, ffn2_w_down, ple_norm, ple_gate_w, ple_proj_w):
    h = x.reshape(N_TOK, D_MODEL)
    for i in range(DEPTH):
        w = _prep_layer(i, ffn1_norm, ffn1_w_gate, ffn1_w_up, ffn1_w_down, mix_norm, w_in, m_conv_w,
                        m_conv_b, m_igate_b, m_fgate_b, m_out_norm, a_q_norm, a_k_norm, a_sinks,
                        w_branch_m, w_branch_a, w_out, ffn2_norm, ffn2_w_gate, ffn2_w_up, ffn2_w_down,
                        ple_norm, ple_gate_w, ple_proj_w)
        h = _ffn_call(h, *w["ffn1"])
        mq, mk, mv, sog, ga, gb, gm, grow, aq, ak, av, sgm, sga = _inproj_call(h, w)
        hm = _mlstm_call(mq, mk, mv, sog, ga, gb, gm, grow)
        ha = _swa_call(aq, ak, av, w["sinks"])
        h = _merge_call(h, hm, ha, sgm, sga, w["wm"], w["wa"], w["wout"])
        pg, wpg, wpp = w["ple"]
        h = _ffn_call(h, *w["ffn2"], ple=(p[i].reshape(N_TOK, PLE_DIM), pg, wpg, wpp))
    return h.reshape(BATCH, SEQ, D_MODEL)
```

```python
import math

import jax
import jax.numpy as jnp
import numpy as np
from jax import lax
from jax.experimental import pallas as pl
from jax.experimental.pallas import tpu as pltpu

D_MODEL = 1024
BATCH = 8
SEQ = 4096
DEPTH = 2
PLE_DIM = 256
D_FF = 2752
M_HEADS = 4
M_HEAD_DIM = 256
M_WIDTH = M_HEADS * M_HEAD_DIM
CONV_K = 4
A_Q_HEADS = 16
A_KV_HEADS = 4
A_HEAD_DIM = 64
A_GROUP = A_Q_HEADS // A_KV_HEADS
A_WIDTH = A_Q_HEADS * A_HEAD_DIM
A_KV_WIDTH = A_KV_HEADS * A_HEAD_DIM
WINDOW = 128
A_BLOCK = 128
EPS = 1e-6

N_TOK = BATCH * SEQ
LANES = 128
SUBLANES = 8
MXU_DIM = 256
D_FF_PAD = -(-D_FF // MXU_DIM) * MXU_DIM
TM = 512
M_CHUNK = 256
M_ROWS = 128
A_KV_DUP = A_KV_HEADS * LANES
VMEM_LIMIT = 56 * 1024 * 1024
LOG2E = math.log2(math.e)

BF16 = jnp.bfloat16
F32 = jnp.float32


def _layer_spec(layer, block, index=None):
    index = (layer,) + tuple(index if index is not None else (0,) * len(block))
    return pl.BlockSpec((None,) + tuple(block), lambda *_: index, pipeline_mode=pl.Buffered(1))


def _rms_scale(x):
    return lax.rsqrt(jnp.mean(x * x, axis=-1, keepdims=True) + EPS)


def _log_sigmoid(x):
    return jnp.minimum(x, 0.0) - jnp.log1p(jnp.exp(-jnp.abs(x)))


def _sigmoid(x):
    return 0.5 * jnp.tanh(0.5 * x) + 0.5


def _dup_heads_lanes(x):
    low = lax.broadcasted_iota(jnp.int32, (x.shape[0], LANES), 1) < A_HEAD_DIM
    out = []
    for j in range(x.shape[1] // LANES):
        pair = x[:, j * LANES:(j + 1) * LANES]
        swapped = pltpu.roll(pair, A_HEAD_DIM, 1)
        out += [jnp.where(low, pair, swapped), jnp.where(low, swapped, pair)]
    return jnp.concatenate(out, axis=1)


def _segment_scan(x, axis, seg, op, fill):
    pos = lax.broadcasted_iota(jnp.int32, x.shape, axis) % seg
    shift = 1
    while shift < seg:
        x = op(x, jnp.where(pos >= shift, pltpu.roll(x, shift, axis), fill))
        shift *= 2
    return x


def _ffn_body(x, g_ref, wg_ref, wu_ref, wd_ref):
    xn = (x * _rms_scale(x) * g_ref[...]).astype(BF16)
    y = jnp.zeros(x.shape, F32)
    for c in range(D_FF_PAD // MXU_DIM):
        cols = slice(c * MXU_DIM, (c + 1) * MXU_DIM)
        gate = jnp.dot(xn, wg_ref[:, cols], preferred_element_type=F32)
        up = jnp.dot(xn, wu_ref[:, cols], preferred_element_type=F32)
        act = (gate * jax.nn.sigmoid(gate) * up).astype(BF16)
        y = y + jnp.dot(act, wd_ref[cols, :], preferred_element_type=F32)
    return x + 0.5 * y


def _ffn_kernel(x_ref, g_ref, wg_ref, wu_ref, wd_ref, o_ref):
    o_ref[...] = _ffn_body(x_ref[...], g_ref, wg_ref, wu_ref, wd_ref)


def _ffn_ple_kernel(x_ref, g_ref, wg_ref, wu_ref, wd_ref, p_ref, pg_ref, wpg_ref, wpp_ref, o_ref):
    h = _ffn_body(x_ref[...], g_ref, wg_ref, wu_ref, wd_ref)
    hn = (h * _rms_scale(h) * pg_ref[...]).astype(BF16)
    gate = jax.nn.sigmoid(jnp.dot(hn, wpg_ref[...], preferred_element_type=F32))
    emb = jnp.dot(p_ref[...].astype(BF16), wpp_ref[...], preferred_element_type=F32)
    o_ref[...] = h + gate * emb


def _ffn_call(h, layer, g, wg, wu, wd, ple=None):
    tok = lambda w: pl.BlockSpec((TM, w), lambda i: (i, 0))
    in_specs = [tok(D_MODEL), _layer_spec(layer, (1, D_MODEL)), _layer_spec(layer, (D_MODEL, D_FF_PAD)),
                _layer_spec(layer, (D_MODEL, D_FF_PAD)), _layer_spec(layer, (D_FF_PAD, D_MODEL))]
    args = [h, g, wg, wu, wd]
    body = _ffn_kernel
    if ple is not None:
        p, pg, wpg, wpp = ple
        in_specs += [pl.BlockSpec((None, TM, PLE_DIM), lambda i: (layer, i, 0)), _layer_spec(layer, (1, D_MODEL)),
                     _layer_spec(layer, (D_MODEL, D_MODEL)), _layer_spec(layer, (PLE_DIM, D_MODEL))]
        args += [p, pg, wpg, wpp]
        body = _ffn_ple_kernel
    return pl.pallas_call(
        body,
        grid=(N_TOK // TM,),
        in_specs=in_specs,
        out_specs=tok(D_MODEL),
        out_shape=jax.ShapeDtypeStruct((N_TOK, D_MODEL), F32),
        compiler_params=pltpu.CompilerParams(dimension_semantics=("arbitrary",),
                                             vmem_limit_bytes=VMEM_LIMIT),
        name="ffn_ple" if ple is not None else "ffn",
    )(*args)


CONV_COLS = 256
CONV_ROWS = 64
ROW_SPLIT = 2
PLAIN_PARTS = (2 * M_WIDTH // CONV_COLS) // 4
PLAIN_COLS = D_MODEL // PLAIN_PARTS


def _inproj_kernel(h_ref, g_ref, wqk_ref, wv_ref, wo_ref, wif_ref, wift_ref, waq_ref, wak_ref, wav_ref,
                   wgm_ref, wga_ref, cw_ref, cb_ref, gbc_ref, gbr_ref, pq_ref, qg_ref, kg_ref, og_ref,
                   mq_ref, mk_ref, mv_ref, sog_ref, ga_ref, gb_ref, gm_ref, grow_ref, aq_ref, ak_ref, av_ref,
                   sgm_ref, sga_ref, *scratch):
    zbufs, stages = scratch[:-2], scratch[-2:]
    i = pl.program_id(0)

    @pl.when(i % (SEQ // TM) == 0)
    def _():
        for zb in zbufs:
            zb[0:SUBLANES, :] = jnp.zeros((SUBLANES, CONV_COLS), F32)

    x = h_ref[...]
    u = (x * _rms_scale(x) * g_ref[...]).astype(BF16)

    n_groups = 2 * M_WIDTH // CONV_COLS

    half = TM // ROW_SPLIT
    periods = [(c, hf) for c in range(n_groups) for hf in range(ROW_SPLIT)]

    def issue_dots(idx):
        c, hf = periods[idx]
        uh = u[hf * half:(hf + 1) * half]
        cols = slice(c * CONV_COLS, (c + 1) * CONV_COLS)
        zbufs[c][SUBLANES + hf * half:SUBLANES + (hf + 1) * half, :] = jnp.dot(
            uh, wqk_ref[:, cols], preferred_element_type=F32)
        which, part = divmod(c, PLAIN_PARTS)
        pc = slice(part * PLAIN_COLS, (part + 1) * PLAIN_COLS)
        stages[idx % 2][...] = jnp.dot(uh, (wv_ref, wo_ref, wgm_ref, wga_ref)[which][:, pc],
                                       preferred_element_type=F32)

    issue_dots(0)
    for idx, (c, hf) in enumerate(periods):
        if idx + 1 < len(periods):
            issue_dots(idx + 1)
        cols = slice(c * CONV_COLS, (c + 1) * CONV_COLS)
        which, part = divmod(c, PLAIN_PARTS)
        pc = slice(part * PLAIN_COLS, (part + 1) * PLAIN_COLS)
        st, zb = stages[idx % 2], zbufs[c]
        for rb in range(half // CONV_ROWS):
            row0 = hf * half + rb * CONV_ROWS
            win = zb[row0:row0 + SUBLANES + CONV_ROWS, :]
            win1 = pltpu.roll(win, 1, 0)
            near = win * cw_ref[3:4, cols] + win1 * cw_ref[2:3, cols]
            far = win * cw_ref[1:2, cols] + win1 * cw_ref[0:1, cols]
            acc = (near + pltpu.roll(far, 2, 0))[SUBLANES:] + cb_ref[:, cols]
            y = acc * _sigmoid(acc)
            rows = slice(row0, row0 + CONV_ROWS)
            if c < M_WIDTH // CONV_COLS:
                mq_ref[rows, cols] = (y * (M_HEAD_DIM ** -0.5)).astype(BF16)
            else:
                mk_ref[rows, slice(cols.start - M_WIDTH, cols.stop - M_WIDTH)] = y.astype(BF16)
        if hf == ROW_SPLIT - 1:
            zb[0:SUBLANES, :] = zb[TM:TM + SUBLANES, :]
        for rb in range(half // CONV_ROWS):
            rows = slice(hf * half + rb * CONV_ROWS, hf * half + (rb + 1) * CONV_ROWS)
            r = st[rb * CONV_ROWS:(rb + 1) * CONV_ROWS, :]
            if which == 0:
                mv_ref[rows, pc] = r.astype(BF16)
            elif which == 1:
                sog_ref[rows, pc] = (_sigmoid(r) * og_ref[:, pc]).astype(BF16)
            elif which == 2:
                sgm_ref[rows, pc] = _sigmoid(r).astype(BF16)
            else:
                sga_ref[rows, pc] = _sigmoid(r).astype(BF16)

    zc = jnp.dot(u, wif_ref[...], preferred_element_type=F32) + gbc_ref[...]
    b_col = _segment_scan(_log_sigmoid(zc[:, LANES:]) * LOG2E, 0, M_CHUNK, jnp.add, 0.0)
    a_col = zc[:, :LANES] * LOG2E - b_col
    ga_ref[...] = a_col
    gb_ref[...] = b_col
    gm_ref[...] = _segment_scan(a_col, 0, M_CHUNK, jnp.maximum, -jnp.inf)
    zr = lax.dot_general(wift_ref[...], u, (((1,), (1,)), ((), ())), preferred_element_type=F32) + gbr_ref[...]
    b_row = _segment_scan(_log_sigmoid(zr[SUBLANES:]) * LOG2E, 1, M_CHUNK, jnp.add, 0.0)
    grow_ref[...] = zr[:SUBLANES] * LOG2E - b_row

    head_sum = pq_ref[...]

    def head_norm(z):
        ss = jnp.dot((z * z).astype(BF16), head_sum, preferred_element_type=F32)
        return z * lax.rsqrt(ss * (1.0 / A_HEAD_DIM) + EPS)

    zq = jnp.dot(u, waq_ref[...], preferred_element_type=F32)
    for g in range(A_WIDTH // MXU_DIM):
        gc = slice(g * MXU_DIM, (g + 1) * MXU_DIM)
        aq_ref[:, gc] = (head_norm(zq[:, gc]) * qg_ref[:, gc]).astype(BF16)
    zk = jnp.dot(u, wak_ref[...], preferred_element_type=F32)
    ak_ref[...] = _dup_heads_lanes(head_norm(zk) * kg_ref[...]).astype(BF16)
    av_ref[...] = _dup_heads_lanes(jnp.dot(u, wav_ref[...], preferred_element_type=F32)).astype(BF16)


def _inproj_call(h, layer, w):
    tok = lambda width: pl.BlockSpec((TM, width), lambda i: (i, 0))
    w_all = w["w_in"]

    def group(name):
        off, width = W_IN_OFFS[name]
        return w_all, _layer_spec(layer, (D_MODEL, width), (0, off // width))

    whole = lambda name: (w[name], _layer_spec(layer, w[name].shape[1:]))
    consts = [whole("mix_norm"), group("qk"), group("v"), group("o"), whole("wif"), whole("wift"), group("aq"),
              group("ak"), group("av"), group("gm"), group("ga"), whole("conv_w"), whole("conv_b"),
              whole("gate_b_col"), whole("gate_b_row"),
              (w["head_sum"], pl.BlockSpec((MXU_DIM, MXU_DIM), lambda i: (0, 0), pipeline_mode=pl.Buffered(1))),
              whole("q_gain"), whole("k_gain"), whole("out_gain")]
    out_shape = [
        jax.ShapeDtypeStruct((N_TOK, M_WIDTH), BF16),
        jax.ShapeDtypeStruct((N_TOK, M_WIDTH), BF16),
        jax.ShapeDtypeStruct((N_TOK, M_WIDTH), BF16),
        jax.ShapeDtypeStruct((N_TOK, M_WIDTH), BF16),
        jax.ShapeDtypeStruct((N_TOK, LANES), F32),
        jax.ShapeDtypeStruct((N_TOK, LANES), F32),
        jax.ShapeDtypeStruct((N_TOK, LANES), F32),
        jax.ShapeDtypeStruct((SUBLANES, N_TOK), F32),
        jax.ShapeDtypeStruct((N_TOK, A_WIDTH), BF16),
        jax.ShapeDtypeStruct((N_TOK, A_KV_DUP), BF16),
        jax.ShapeDtypeStruct((N_TOK, A_KV_DUP), BF16),
        jax.ShapeDtypeStruct((N_TOK, D_MODEL), BF16),
        jax.ShapeDtypeStruct((N_TOK, D_MODEL), BF16),
    ]
    out_specs = [tok(M_WIDTH), tok(M_WIDTH), tok(M_WIDTH), tok(M_WIDTH), tok(LANES), tok(LANES), tok(LANES),
                 pl.BlockSpec((SUBLANES, TM), lambda i: (0, i)),
                 tok(A_WIDTH), tok(A_KV_DUP), tok(A_KV_DUP), tok(D_MODEL), tok(D_MODEL)]
    return pl.pallas_call(
        _inproj_kernel,
        grid=(N_TOK // TM,),
        in_specs=[tok(D_MODEL)] + [spec for _, spec in consts],
        out_specs=out_specs,
        out_shape=out_shape,
        scratch_shapes=([pltpu.VMEM((SUBLANES + TM, CONV_COLS), F32)] * (2 * M_WIDTH // CONV_COLS)
                        + [pltpu.VMEM((TM // ROW_SPLIT, PLAIN_COLS), F32)] * 2),
        compiler_params=pltpu.CompilerParams(dimension_semantics=("arbitrary",),
                                             vmem_limit_bytes=VMEM_LIMIT),
        name="inproj",
    )(h, *[arr for arr, _ in consts])


def _mlstm_kernel(q_ref, k_ref, v_ref, sog_ref, ga_ref, gb_ref, gm_ref, grow_ref, o_ref, c_ref, n_ref, m_ref):
    L, R = M_CHUNK, M_ROWS

    @pl.when(pl.program_id(1) == 0)
    def _():
        c_ref[...] = jnp.zeros(c_ref.shape, F32)
        n_ref[...] = jnp.zeros(n_ref.shape, F32)
        m_ref[...] = jnp.zeros(m_ref.shape, F32)

    tri = lax.broadcasted_iota(jnp.int32, (R, R), 0) >= lax.broadcasted_iota(jnp.int32, (R, R), 1)
    lane = lax.broadcasted_iota(jnp.int32, (L, LANES), 1)
    ga, gb, gm = ga_ref[...], gb_ref[...], gm_ref[...]
    for hd in range(M_HEADS):
        cols = slice(hd * M_HEAD_DIM, (hd + 1) * M_HEAD_DIM)
        a_row = grow_ref[hd:hd + 1, :]
        pick = lane == hd
        a_col = jnp.sum(jnp.where(pick, ga, 0.0), axis=-1, keepdims=True)
        b_col = jnp.sum(jnp.where(pick, gb, 0.0), axis=-1, keepdims=True)
        cm_col = jnp.sum(jnp.where(pick, gm, 0.0), axis=-1, keepdims=True)
        m_prev = jnp.max(m_ref[hd:hd + 1, :], axis=-1, keepdims=True)
        n_prev = n_ref[hd:hd + 1, :]
        ck_prev = c_ref[hd]
        ck_bf = ck_prev.astype(BF16)
        m_loc = jnp.maximum(cm_col, m_prev)
        w_inter = jnp.exp2(m_prev - m_loc)
        floor = jnp.exp2(-(b_col + m_loc))
        k = k_ref[:, cols]
        v = v_ref[:, cols]
        for rb in range(L // R):
            rows = slice(rb * R, (rb + 1) * R)
            width = (rb + 1) * R
            q = q_ref[rows, cols]
            s = lax.dot_general(q, k[:width], (((1,), (1,)), ((), ())), preferred_element_type=F32)
            ml = m_loc[rows]
            wi = w_inter[rows]
            blocks = []
            for cb in range(rb + 1):
                e = jnp.exp2(a_row[:, cb * R:(cb + 1) * R] - ml)
                if cb == rb:
                    e = jnp.where(tri, e, 0.0)
                blocks.append(s[:, cb * R:(cb + 1) * R] * e)
            sw = blocks[0] if rb == 0 else jnp.concatenate(blocks, axis=1)
            num = (jnp.dot(sw.astype(BF16), v[:width], preferred_element_type=F32)
                   + wi * jnp.dot(q, ck_bf, preferred_element_type=F32))
            qn = q.astype(F32) * n_prev
            dsum = wi * (qn[:, :LANES] + qn[:, LANES:])
            for blk in blocks:
                dsum = dsum + blk
            den = jnp.sum(dsum, axis=-1, keepdims=True)
            dd = jnp.maximum(jnp.abs(den), floor[rows])
            scale = lax.rsqrt(jnp.mean(num * num, axis=-1, keepdims=True) + EPS * (dd * dd))
            o_ref[rows, cols] = (num * scale * sog_ref[rows, cols]).astype(BF16)

        m_last = m_loc[L - 1:L, :]
        kw = k.astype(F32) * jnp.exp2(a_col - m_last)
        decay = jnp.exp2(m_prev - m_last)
        c_ref[hd] = decay * ck_prev + lax.dot_general(kw.astype(BF16), v, (((0,), (0,)), ((), ())),
                                                      preferred_element_type=F32)
        n_ref[hd:hd + 1, :] = decay * n_prev + jnp.sum(kw, axis=0, keepdims=True)
        m_ref[hd:hd + 1, :] = jnp.broadcast_to(b_col[L - 1:L, :] + m_last, (1, LANES))


def _mlstm_call(mq, mk, mv, sog, ga, gb, gm, grow):
    nc = SEQ // M_CHUNK
    tok = lambda width: pl.BlockSpec((M_CHUNK, width), lambda b, c: (b * nc + c, 0))
    return pl.pallas_call(
        _mlstm_kernel,
        grid=(BATCH, nc),
        in_specs=[tok(M_WIDTH), tok(M_WIDTH), tok(M_WIDTH), tok(M_WIDTH), tok(LANES), tok(LANES), tok(LANES),
                  pl.BlockSpec((SUBLANES, M_CHUNK), lambda b, c: (0, b * nc + c))],
        out_specs=tok(M_WIDTH),
        out_shape=jax.ShapeDtypeStruct((N_TOK, M_WIDTH), BF16),
        scratch_shapes=[pltpu.VMEM((M_HEADS, M_HEAD_DIM, M_HEAD_DIM), F32),
                        pltpu.VMEM((SUBLANES, M_HEAD_DIM), F32),
                        pltpu.VMEM((SUBLANES, LANES), F32)],
        compiler_params=pltpu.CompilerParams(dimension_semantics=("arbitrary", "arbitrary"),
                                             vmem_limit_bytes=VMEM_LIMIT),
        name="mlstm",
    )(mq, mk, mv, sog, ga, gb, gm, grow)


def _swa_kernel(q_ref, kp_ref, kc_ref, vp_ref, vc_ref, sink_ref, o_ref):
    n = pl.program_id(1)
    blk = A_BLOCK
    tri = lax.broadcasted_iota(jnp.int32, (blk, blk), 0) >= lax.broadcasted_iota(jnp.int32, (blk, blk), 1)
    prev_bias = jnp.where(n > 0, 0.0, -jnp.inf)
    low_kv = lax.broadcasted_iota(jnp.int32, (2 * blk, LANES), 1) < A_HEAD_DIM
    low_out = lax.broadcasted_iota(jnp.int32, (blk, LANES), 1) < A_HEAD_DIM
    ones = jnp.ones((2 * blk, LANES), BF16)
    zero_kv = jnp.zeros((2 * blk, LANES), BF16)
    for hk in range(A_KV_HEADS):
        kvcols = slice(hk * LANES, (hk + 1) * LANES)
        kd = jnp.concatenate([kc_ref[:, kvcols], kp_ref[:, kvcols]], axis=0)
        vd = jnp.concatenate([vc_ref[:, kvcols], vp_ref[:, kvcols]], axis=0)
        k_half = (jnp.where(low_kv, kd, zero_kv), jnp.where(low_kv, zero_kv, kd))
        v_ext = jnp.concatenate([vd, ones], axis=1)
        pairs = [hk * (A_GROUP // 2) + j for j in range(A_GROUP // 2)]
        qcat = jnp.concatenate([q_ref[:, pr * LANES:(pr + 1) * LANES] for pr in pairs], axis=0)
        probs, sink_w = {}, {}
        for half in range(2):
            s2 = lax.dot_general(qcat, k_half[half], (((1,), (1,)), ((), ())), preferred_element_type=F32)
            for j, pr in enumerate(pairs):
                hq = 2 * pr + half
                sj = s2[j * blk:(j + 1) * blk]
                s = jnp.where(tri, sj[:, :blk], sj[:, blk:] + prev_bias)
                sink = sink_ref[hq:hq + 1, 0:1]
                mx = jnp.maximum(jnp.broadcast_to(jnp.max(s, axis=-1, keepdims=True), (blk, blk)), sink)
                p = jnp.exp2(s - mx)
                sink_w[hq] = jnp.exp2(sink - mx)
                probs[hq] = jnp.concatenate([jnp.where(tri, p, 0.0), jnp.where(tri, 0.0, p)], axis=1).astype(BF16)
        heads = sorted(probs)
        o2 = jnp.dot(jnp.concatenate([probs[hq] for hq in heads], axis=0), v_ext,
                     preferred_element_type=F32)
        outs = {}
        for r, hq in enumerate(heads):
            oh = o2[r * blk:(r + 1) * blk]
            outs[hq] = oh[:, :LANES] / (oh[:, LANES:] + sink_w[hq])
        for pr in pairs:
            o_ref[:, pr * LANES:(pr + 1) * LANES] = jnp.where(low_out, outs[2 * pr], outs[2 * pr + 1]).astype(BF16)


def _swa_call(aq, ak, av, layer, sinks):
    nb = SEQ // A_BLOCK
    cur = lambda width: pl.BlockSpec((A_BLOCK, width), lambda b, n: (b * nb + n, 0))
    prev = lambda width: pl.BlockSpec((A_BLOCK, width), lambda b, n: (b * nb + jnp.maximum(n - 1, 0), 0))
    return pl.pallas_call(
        _swa_kernel,
        grid=(BATCH, nb),
        in_specs=[cur(A_WIDTH), prev(A_KV_DUP), cur(A_KV_DUP), prev(A_KV_DUP), cur(A_KV_DUP),
                  _layer_spec(layer, (A_Q_HEADS, LANES))],
        out_specs=cur(A_WIDTH),
        out_shape=jax.ShapeDtypeStruct((N_TOK, A_WIDTH), BF16),
        compiler_params=pltpu.CompilerParams(dimension_semantics=("arbitrary", "arbitrary"),
                                             vmem_limit_bytes=VMEM_LIMIT),
        name="swa",
    )(aq, ak, ak, av, av, sinks)


def _merge_kernel(h_ref, hm_ref, ha_ref, sgm_ref, sga_ref, wm_ref, wa_ref, wo_ref, o_ref):
    bm = jnp.dot(hm_ref[...], wm_ref[...], preferred_element_type=F32)
    ba = jnp.dot(ha_ref[...], wa_ref[...], preferred_element_type=F32)
    merged = (sgm_ref[...] * bm + sga_ref[...] * ba).astype(BF16)
    o_ref[...] = h_ref[...] + jnp.dot(merged, wo_ref[...], preferred_element_type=F32)


def _merge_call(h, hm, ha, sgm, sga, layer, wm, wa, wo):
    tok = pl.BlockSpec((TM, D_MODEL), lambda i: (i, 0))
    wspec = _layer_spec(layer, (D_MODEL, D_MODEL))
    return pl.pallas_call(
        _merge_kernel,
        grid=(N_TOK // TM,),
        in_specs=[tok, tok, tok, tok, tok, wspec, wspec, wspec],
        out_specs=tok,
        out_shape=jax.ShapeDtypeStruct((N_TOK, D_MODEL), F32),
        compiler_params=pltpu.CompilerParams(dimension_semantics=("arbitrary",),
                                             vmem_limit_bytes=VMEM_LIMIT),
        name="merge",
    )(h, hm, ha, sgm, sga, wm, wa, wo)


W_IN_OFFS = {"qk": (0, 2 * M_WIDTH), "v": (2048, M_WIDTH), "o": (3072, M_WIDTH), "gm": (4096, D_MODEL),
             "ga": (5120, D_MODEL), "aq": (6144, A_WIDTH), "ak": (7168, A_KV_WIDTH), "av": (7424, A_KV_WIDTH)}


def _head_sum_matrix():
    idx = np.arange(MXU_DIM) // A_HEAD_DIM
    return jnp.asarray(idx[:, None] == idx[None, :], dtype=BF16)


def _prep(ffn1_norm, ffn1_w_gate, ffn1_w_up, ffn1_w_down, mix_norm, w_in, m_conv_w, m_conv_b,
          m_igate_b, m_fgate_b, m_out_norm, a_q_norm, a_k_norm, a_sinks, w_branch_m, w_branch_a,
          w_out, ffn2_norm, ffn2_w_gate, ffn2_w_up, ffn2_w_down, ple_norm, ple_gate_w, ple_proj_w):
    ff_pad = D_FF_PAD - D_FF
    row = lambda v: v.reshape(DEPTH, 1, -1).astype(F32)
    wide = lambda wt: jnp.pad(wt, ((0, 0), (0, 0), (0, ff_pad))).astype(BF16)
    tall = lambda wt: jnp.pad(wt, ((0, 0), (0, ff_pad), (0, 0))).astype(BF16)
    sizes = [2 * M_WIDTH, M_WIDTH, M_WIDTH, M_HEADS, M_HEADS, A_WIDTH, A_KV_WIDTH, A_KV_WIDTH, D_MODEL, D_MODEL]
    offs = np.concatenate([[0], np.cumsum(sizes)])
    part = lambda j: w_in[:, :, int(offs[j]):int(offs[j + 1])]
    w_all = jnp.concatenate([part(0), part(1), part(2), part(8), part(9), part(5), part(6), part(7)],
                            axis=-1).astype(BF16)
    lane_pad = lambda wt: jnp.pad(wt, ((0, 0), (0, 0), (0, LANES - M_HEADS)))
    sub_pad = lambda wt: jnp.pad(wt, ((0, 0), (0, SUBLANES - M_HEADS), (0, 0)))
    w_i, w_f = part(3).astype(BF16), part(4).astype(BF16)
    b_i, b_f = m_igate_b.astype(F32), m_fgate_b.astype(F32)
    return dict(
        ffn1=(row(ffn1_norm), wide(ffn1_w_gate), wide(ffn1_w_up), tall(ffn1_w_down)),
        ffn2=(row(ffn2_norm), wide(ffn2_w_gate), wide(ffn2_w_up), tall(ffn2_w_down)),
        ple=(row(ple_norm), ple_gate_w.astype(BF16), ple_proj_w.astype(BF16)),
        mix_norm=row(mix_norm), w_in=w_all,
        wif=jnp.concatenate([lane_pad(w_i), lane_pad(w_f)], axis=-1),
        wift=jnp.concatenate([sub_pad(jnp.swapaxes(w_i, 1, 2)), sub_pad(jnp.swapaxes(w_f, 1, 2))], axis=1),
        gate_b_col=jnp.concatenate([lane_pad(b_i[:, None, :]), lane_pad(b_f[:, None, :])], axis=-1),
        gate_b_row=jnp.concatenate([sub_pad(b_i[:, :, None]), sub_pad(b_f[:, :, None])], axis=1),
        conv_w=m_conv_w.astype(F32), conv_b=row(m_conv_b), head_sum=_head_sum_matrix(),
        q_gain=row(jnp.tile(a_q_norm, (1, A_Q_HEADS)) * (A_HEAD_DIM ** -0.5 * LOG2E)),
        k_gain=row(jnp.tile(a_k_norm, (1, A_KV_HEADS))),
        out_gain=row(m_out_norm),
        sinks=jnp.broadcast_to((a_sinks.astype(F32) * LOG2E)[:, :, None], (DEPTH, A_Q_HEADS, LANES)),
        wm=w_branch_m.astype(BF16), wa=w_branch_a.astype(BF16), wout=w_out.astype(BF16),
    )


def kernel(x, p, ffn1_norm, ffn1_w_gate, ffn1_w_up, ffn1_w_down, mix_norm, w_in, m_conv_w, m_conv_b, m_igate_b, m_fgate_b, m_out_norm, a_q_norm, a_k_norm, a_sinks, w_branch_m, w_branch_a, w_out, ffn2_norm, ffn2_w_gate, ffn2_w_up, ffn2_w_down, ple_norm, ple_gate_w, ple_proj_w):
    w = _prep(ffn1_norm, ffn1_w_gate, ffn1_w_up, ffn1_w_down, mix_norm, w_in, m_conv_w, m_conv_b, m_igate_b,
              m_fgate_b, m_out_norm, a_q_norm, a_k_norm, a_sinks, w_branch_m, w_branch_a, w_out, ffn2_norm,
              ffn2_w_gate, ffn2_w_up, ffn2_w_down, ple_norm, ple_gate_w, ple_proj_w)
    h = x.reshape(N_TOK, D_MODEL)
    p_tok = p.reshape(DEPTH, N_TOK, PLE_DIM)
    for i in range(DEPTH):
        h = _ffn_call(h, i, *w["ffn1"])
        mq, mk, mv, sog, ga, gb, gm, grow, aq, ak, av, sgm, sga = _inproj_call(h, i, w)
        hm = _mlstm_call(mq, mk, mv, sog, ga, gb, gm, grow)
        ha = _swa_call(aq, ak, av, i, w["sinks"])
        h = _merge_call(h, hm, ha, sgm, sga, i, w["wm"], w["wa"], w["wout"])
        pg, wpg, wpp = w["ple"]
        h = _ffn_call(h, i, *w["ffn2"], ple=(p_tok, pg, wpg, wpp))
    return h.reshape(BATCH, SEQ, D_MODEL)
```

```python
import math

import jax
import jax.numpy as jnp
import numpy as np
from jax import lax
from jax.experimental import pallas as pl
from jax.experimental.pallas import tpu as pltpu

D_MODEL = 1024
BATCH = 8
SEQ = 4096
DEPTH = 2
PLE_DIM = 256
D_FF = 2752
M_HEADS = 4
M_HEAD_DIM = 256
M_WIDTH = M_HEADS * M_HEAD_DIM
CONV_K = 4
A_Q_HEADS = 16
A_KV_HEADS = 4
A_HEAD_DIM = 64
A_GROUP = A_Q_HEADS // A_KV_HEADS
A_WIDTH = A_Q_HEADS * A_HEAD_DIM
A_KV_WIDTH = A_KV_HEADS * A_HEAD_DIM
WINDOW = 128
A_BLOCK = 128
EPS = 1e-6

N_TOK = BATCH * SEQ
LANES = 128
SUBLANES = 8
MXU_DIM = 256
D_FF_PAD = -(-D_FF // MXU_DIM) * MXU_DIM
TM = 512
M_CHUNK = 256
M_ROWS = 128
M_STEP_CHUNKS = 2
A_STEP_BLOCKS = 4
A_KV_DUP = A_KV_HEADS * LANES
VMEM_LIMIT = 56 * 1024 * 1024
LOG2E = math.log2(math.e)

BF16 = jnp.bfloat16
F32 = jnp.float32


def _layer_spec(layer, block, index=None):
    index = (layer,) + tuple(index if index is not None else (0,) * len(block))
    return pl.BlockSpec((None,) + tuple(block), lambda *_: index, pipeline_mode=pl.Buffered(1))


def _rms_scale(x):
    return lax.rsqrt(jnp.mean(x * x, axis=-1, keepdims=True) + EPS)


def _log_sigmoid(x):
    return jnp.minimum(x, 0.0) - jnp.log1p(jnp.exp(-jnp.abs(x)))


def _sigmoid(x):
    return 0.5 * jnp.tanh(0.5 * x) + 0.5


def _dup_heads_lanes(x):
    low = lax.broadcasted_iota(jnp.int32, (x.shape[0], LANES), 1) < A_HEAD_DIM
    out = []
    for j in range(x.shape[1] // LANES):
        pair = x[:, j * LANES:(j + 1) * LANES]
        swapped = pltpu.roll(pair, A_HEAD_DIM, 1)
        out += [jnp.where(low, pair, swapped), jnp.where(low, swapped, pair)]
    return jnp.concatenate(out, axis=1)


def _segment_scan(x, axis, seg, op, fill):
    pos = lax.broadcasted_iota(jnp.int32, x.shape, axis) % seg
    shift = 1
    while shift < seg:
        x = op(x, jnp.where(pos >= shift, pltpu.roll(x, shift, axis), fill))
        shift *= 2
    return x


def _ffn_body(x, g_ref, wg_ref, wu_ref, wd_ref):
    xn = (x * _rms_scale(x) * g_ref[...]).astype(BF16)
    y = jnp.zeros(x.shape, F32)
    for c in range(D_FF_PAD // MXU_DIM):
        cols = slice(c * MXU_DIM, (c + 1) * MXU_DIM)
        gate = jnp.dot(xn, wg_ref[:, cols], preferred_element_type=F32)
        up = jnp.dot(xn, wu_ref[:, cols], preferred_element_type=F32)
        act = (gate * jax.nn.sigmoid(gate) * up).astype(BF16)
        y = y + jnp.dot(act, wd_ref[cols, :], preferred_element_type=F32)
    return x + 0.5 * y


def _ffn_kernel(x_ref, g_ref, wg_ref, wu_ref, wd_ref, o_ref):
    o_ref[...] = _ffn_body(x_ref[...], g_ref, wg_ref, wu_ref, wd_ref)


def _ffn_ple_kernel(x_ref, g_ref, wg_ref, wu_ref, wd_ref, p_ref, pg_ref, wpg_ref, wpp_ref, o_ref):
    h = _ffn_body(x_ref[...], g_ref, wg_ref, wu_ref, wd_ref)
    hn = (h * _rms_scale(h) * pg_ref[...]).astype(BF16)
    gate = jax.nn.sigmoid(jnp.dot(hn, wpg_ref[...], preferred_element_type=F32))
    emb = jnp.dot(p_ref[...].astype(BF16), wpp_ref[...], preferred_element_type=F32)
    o_ref[...] = h + gate * emb


def _ffn_call(h, layer, g, wg, wu, wd, ple=None):
    tok = lambda w: pl.BlockSpec((TM, w), lambda i: (i, 0))
    in_specs = [tok(D_MODEL), _layer_spec(layer, (1, D_MODEL)), _layer_spec(layer, (D_MODEL, D_FF_PAD)),
                _layer_spec(layer, (D_MODEL, D_FF_PAD)), _layer_spec(layer, (D_FF_PAD, D_MODEL))]
    args = [h, g, wg, wu, wd]
    body = _ffn_kernel
    if ple is not None:
        p, pg, wpg, wpp = ple
        in_specs += [pl.BlockSpec((None, TM, PLE_DIM), lambda i: (layer, i, 0)), _layer_spec(layer, (1, D_MODEL)),
                     _layer_spec(layer, (D_MODEL, D_MODEL)), _layer_spec(layer, (PLE_DIM, D_MODEL))]
        args += [p, pg, wpg, wpp]
        body = _ffn_ple_kernel
    return pl.pallas_call(
        body,
        grid=(N_TOK // TM,),
        in_specs=in_specs,
        out_specs=tok(D_MODEL),
        out_shape=jax.ShapeDtypeStruct((N_TOK, D_MODEL), F32),
        compiler_params=pltpu.CompilerParams(dimension_semantics=("arbitrary",),
                                             vmem_limit_bytes=VMEM_LIMIT),
        name="ffn_ple" if ple is not None else "ffn",
    )(*args)


CONV_COLS = 256
CONV_ROWS = 64
ROW_SPLIT = 2
PLAIN_PARTS = (2 * M_WIDTH // CONV_COLS) // 4
PLAIN_COLS = D_MODEL // PLAIN_PARTS


def _inproj_kernel(h_ref, g_ref, wqk_ref, wv_ref, wo_ref, wif_ref, wift_ref, waq_ref, wak_ref, wav_ref,
                   wgm_ref, wga_ref, cw_ref, cb_ref, gbc_ref, gbr_ref, pq_ref, qg_ref, kg_ref, og_ref,
                   mq_ref, mk_ref, mv_ref, sog_ref, ga_ref, gb_ref, gm_ref, grow_ref, aq_ref, ak_ref, av_ref,
                   sgm_ref, sga_ref, *scratch):
    zbufs, stages = scratch[:-2], scratch[-2:]
    i = pl.program_id(0)

    @pl.when(i % (SEQ // TM) == 0)
    def _():
        for zb in zbufs:
            zb[0:SUBLANES, :] = jnp.zeros((SUBLANES, CONV_COLS), F32)

    x = h_ref[...]
    u = (x * _rms_scale(x) * g_ref[...]).astype(BF16)

    n_groups = 2 * M_WIDTH // CONV_COLS

    half = TM // ROW_SPLIT
    periods = [(c, hf) for c in range(n_groups) for hf in range(ROW_SPLIT)]

    def issue_dots(idx):
        c, hf = periods[idx]
        uh = u[hf * half:(hf + 1) * half]
        cols = slice(c * CONV_COLS, (c + 1) * CONV_COLS)
        zbufs[c][SUBLANES + hf * half:SUBLANES + (hf + 1) * half, :] = jnp.dot(
            uh, wqk_ref[:, cols], preferred_element_type=F32)
        which, part = divmod(c, PLAIN_PARTS)
        pc = slice(part * PLAIN_COLS, (part + 1) * PLAIN_COLS)
        stages[idx % 2][...] = jnp.dot(uh, (wv_ref, wo_ref, wgm_ref, wga_ref)[which][:, pc],
                                       preferred_element_type=F32)

    issue_dots(0)
    for idx, (c, hf) in enumerate(periods):
        if idx + 1 < len(periods):
            issue_dots(idx + 1)
        cols = slice(c * CONV_COLS, (c + 1) * CONV_COLS)
        which, part = divmod(c, PLAIN_PARTS)
        pc = slice(part * PLAIN_COLS, (part + 1) * PLAIN_COLS)
        st, zb = stages[idx % 2], zbufs[c]
        for rb in range(half // CONV_ROWS):
            row0 = hf * half + rb * CONV_ROWS
            win = zb[row0:row0 + SUBLANES + CONV_ROWS, :]
            win1 = pltpu.roll(win, 1, 0)
            near = win * cw_ref[3:4, cols] + win1 * cw_ref[2:3, cols]
            far = win * cw_ref[1:2, cols] + win1 * cw_ref[0:1, cols]
            acc = (near + pltpu.roll(far, 2, 0))[SUBLANES:] + cb_ref[:, cols]
            half_acc = 0.5 * acc
            y = half_acc * jnp.tanh(half_acc) + half_acc
            rows = slice(row0, row0 + CONV_ROWS)
            if c < M_WIDTH // CONV_COLS:
                mq_ref[rows, cols] = (y * (M_HEAD_DIM ** -0.5)).astype(BF16)
            else:
                mk_ref[rows, slice(cols.start - M_WIDTH, cols.stop - M_WIDTH)] = y.astype(BF16)
        if hf == ROW_SPLIT - 1:
            zb[0:SUBLANES, :] = zb[TM:TM + SUBLANES, :]
        for rb in range(half // CONV_ROWS):
            rows = slice(hf * half + rb * CONV_ROWS, hf * half + (rb + 1) * CONV_ROWS)
            r = st[rb * CONV_ROWS:(rb + 1) * CONV_ROWS, :]
            if which == 0:
                mv_ref[rows, pc] = r.astype(BF16)
            elif which == 1:
                sog_ref[rows, pc] = (_sigmoid(r) * og_ref[:, pc]).astype(BF16)
            elif which == 2:
                sgm_ref[rows, pc] = _sigmoid(r).astype(BF16)
            else:
                sga_ref[rows, pc] = _sigmoid(r).astype(BF16)

    zc = jnp.dot(u, wif_ref[...], preferred_element_type=F32) + gbc_ref[...]
    b_col = _segment_scan(_log_sigmoid(zc[:, LANES:]) * LOG2E, 0, M_CHUNK, jnp.add, 0.0)
    a_col = zc[:, :LANES] * LOG2E - b_col
    ga_ref[...] = a_col
    gb_ref[...] = b_col
    gm_ref[...] = _segment_scan(a_col, 0, M_CHUNK, jnp.maximum, -jnp.inf)
    zr = lax.dot_general(wift_ref[...], u, (((1,), (1,)), ((), ())), preferred_element_type=F32) + gbr_ref[...]
    b_row = _segment_scan(_log_sigmoid(zr[SUBLANES:]) * LOG2E, 1, M_CHUNK, jnp.add, 0.0)
    grow_ref[...] = zr[:SUBLANES] * LOG2E - b_row

    head_sum = pq_ref[...]

    def head_norm(z):
        ss = jnp.dot((z * z).astype(BF16), head_sum, preferred_element_type=F32)
        return z * lax.rsqrt(ss * (1.0 / A_HEAD_DIM) + EPS)

    zq = jnp.dot(u, waq_ref[...], preferred_element_type=F32)
    for g in range(A_WIDTH // MXU_DIM):
        gc = slice(g * MXU_DIM, (g + 1) * MXU_DIM)
        aq_ref[:, gc] = (head_norm(zq[:, gc]) * qg_ref[:, gc]).astype(BF16)
    zk = jnp.dot(u, wak_ref[...], preferred_element_type=F32)
    ak_ref[...] = _dup_heads_lanes(head_norm(zk) * kg_ref[...]).astype(BF16)
    av_ref[...] = _dup_heads_lanes(jnp.dot(u, wav_ref[...], preferred_element_type=F32)).astype(BF16)


def _inproj_call(h, layer, w):
    tok = lambda width: pl.BlockSpec((TM, width), lambda i: (i, 0))
    w_all = w["w_in"]

    def group(name):
        off, width = W_IN_OFFS[name]
        return w_all, _layer_spec(layer, (D_MODEL, width), (0, off // width))

    whole = lambda name: (w[name], _layer_spec(layer, w[name].shape[1:]))
    consts = [whole("mix_norm"), group("qk"), group("v"), group("o"), whole("wif"), whole("wift"), group("aq"),
              group("ak"), group("av"), group("gm"), group("ga"), whole("conv_w"), whole("conv_b"),
              whole("gate_b_col"), whole("gate_b_row"),
              (w["head_sum"], pl.BlockSpec((MXU_DIM, MXU_DIM), lambda i: (0, 0), pipeline_mode=pl.Buffered(1))),
              whole("q_gain"), whole("k_gain"), whole("out_gain")]
    out_shape = [
        jax.ShapeDtypeStruct((N_TOK, M_WIDTH), BF16),
        jax.ShapeDtypeStruct((N_TOK, M_WIDTH), BF16),
        jax.ShapeDtypeStruct((N_TOK, M_WIDTH), BF16),
        jax.ShapeDtypeStruct((N_TOK, M_WIDTH), BF16),
        jax.ShapeDtypeStruct((N_TOK, LANES), F32),
        jax.ShapeDtypeStruct((N_TOK, LANES), F32),
        jax.ShapeDtypeStruct((N_TOK, LANES), F32),
        jax.ShapeDtypeStruct((SUBLANES, N_TOK), F32),
        jax.ShapeDtypeStruct((N_TOK, A_WIDTH), BF16),
        jax.ShapeDtypeStruct((N_TOK, A_KV_DUP), BF16),
        jax.ShapeDtypeStruct((N_TOK, A_KV_DUP), BF16),
        jax.ShapeDtypeStruct((N_TOK, D_MODEL), BF16),
        jax.ShapeDtypeStruct((N_TOK, D_MODEL), BF16),
    ]
    out_specs = [tok(M_WIDTH), tok(M_WIDTH), tok(M_WIDTH), tok(M_WIDTH), tok(LANES), tok(LANES), tok(LANES),
                 pl.BlockSpec((SUBLANES, TM), lambda i: (0, i)),
                 tok(A_WIDTH), tok(A_KV_DUP), tok(A_KV_DUP), tok(D_MODEL), tok(D_MODEL)]
    return pl.pallas_call(
        _inproj_kernel,
        grid=(N_TOK // TM,),
        in_specs=[tok(D_MODEL)] + [spec for _, spec in consts],
        out_specs=out_specs,
        out_shape=out_shape,
        scratch_shapes=([pltpu.VMEM((SUBLANES + TM, CONV_COLS), F32)] * (2 * M_WIDTH // CONV_COLS)
                        + [pltpu.VMEM((TM // ROW_SPLIT, PLAIN_COLS), F32)] * 2),
        compiler_params=pltpu.CompilerParams(dimension_semantics=("arbitrary",),
                                             vmem_limit_bytes=VMEM_LIMIT),
        name="inproj",
    )(h, *[arr for arr, _ in consts])


def _mlstm_kernel(q_ref, k_ref, v_ref, sog_ref, ga_ref, gb_ref, gm_ref, grow_ref, o_ref, c_ref, n_ref, m_ref):
    L, R = M_CHUNK, M_ROWS

    @pl.when(pl.program_id(1) == 0)
    def _():
        c_ref[...] = jnp.zeros(c_ref.shape, F32)
        n_ref[...] = jnp.zeros(n_ref.shape, F32)
        m_ref[...] = jnp.zeros(m_ref.shape, F32)

    tri = lax.broadcasted_iota(jnp.int32, (R, R), 0) >= lax.broadcasted_iota(jnp.int32, (R, R), 1)
    lane = lax.broadcasted_iota(jnp.int32, (L, LANES), 1)
    for ch in range(M_STEP_CHUNKS):
        _mlstm_chunk(ch * L, tri, lane, q_ref, k_ref, v_ref, sog_ref, ga_ref, gb_ref, gm_ref, grow_ref, o_ref,
                     c_ref, n_ref, m_ref)


def _mlstm_chunk(base, tri, lane, q_ref, k_ref, v_ref, sog_ref, ga_ref, gb_ref, gm_ref, grow_ref, o_ref,
                 c_ref, n_ref, m_ref):
    L, R = M_CHUNK, M_ROWS
    chunk = slice(base, base + L)
    ga, gb, gm = ga_ref[chunk, :], gb_ref[chunk, :], gm_ref[chunk, :]
    for hd in range(M_HEADS):
        cols = slice(hd * M_HEAD_DIM, (hd + 1) * M_HEAD_DIM)
        a_row = grow_ref[hd:hd + 1, chunk]
        pick = lane == hd
        a_col = jnp.sum(jnp.where(pick, ga, 0.0), axis=-1, keepdims=True)
        b_col = jnp.sum(jnp.where(pick, gb, 0.0), axis=-1, keepdims=True)
        cm_col = jnp.sum(jnp.where(pick, gm, 0.0), axis=-1, keepdims=True)
        m_prev = jnp.max(m_ref[hd:hd + 1, :], axis=-1, keepdims=True)
        n_prev = n_ref[hd:hd + 1, :]
        ck_prev = c_ref[hd]
        ck_bf = ck_prev.astype(BF16)
        m_loc = jnp.maximum(cm_col, m_prev)
        w_inter = jnp.exp2(m_prev - m_loc)
        floor = jnp.exp2(-(b_col + m_loc))
        k = k_ref[chunk, cols]
        v = v_ref[chunk, cols]
        for rb in range(L // R):
            rows = slice(rb * R, (rb + 1) * R)
            orows = slice(base + rb * R, base + (rb + 1) * R)
            width = (rb + 1) * R
            q = q_ref[orows, cols]
            s = lax.dot_general(q, k[:width], (((1,), (1,)), ((), ())), preferred_element_type=F32)
            ml = m_loc[rows]
            wi = w_inter[rows]
            blocks = []
            for cb in range(rb + 1):
                e = jnp.exp2(a_row[:, cb * R:(cb + 1) * R] - ml)
                if cb == rb:
                    e = jnp.where(tri, e, 0.0)
                blocks.append(s[:, cb * R:(cb + 1) * R] * e)
            sw = blocks[0] if rb == 0 else jnp.concatenate(blocks, axis=1)
            num = (jnp.dot(sw.astype(BF16), v[:width], preferred_element_type=F32)
                   + wi * jnp.dot(q, ck_bf, preferred_element_type=F32))
            qn = q.astype(F32) * n_prev
            dsum = wi * (qn[:, :LANES] + qn[:, LANES:])
            for blk in blocks:
                dsum = dsum + blk
            den = jnp.sum(dsum, axis=-1, keepdims=True)
            dd = jnp.maximum(jnp.abs(den), floor[rows])
            scale = lax.rsqrt(jnp.mean(num * num, axis=-1, keepdims=True) + EPS * (dd * dd))
            o_ref[orows, cols] = (num * scale * sog_ref[orows, cols]).astype(BF16)

        m_last = m_loc[L - 1:L, :]
        kw = k.astype(F32) * jnp.exp2(a_col - m_last)
        decay = jnp.exp2(m_prev - m_last)
        c_ref[hd] = decay * ck_prev + lax.dot_general(kw.astype(BF16), v, (((0,), (0,)), ((), ())),
                                                      preferred_element_type=F32)
        n_ref[hd:hd + 1, :] = decay * n_prev + jnp.sum(kw, axis=0, keepdims=True)
        m_ref[hd:hd + 1, :] = jnp.broadcast_to(b_col[L - 1:L, :] + m_last, (1, LANES))


def _mlstm_call(mq, mk, mv, sog, ga, gb, gm, grow):
    rows = M_STEP_CHUNKS * M_CHUNK
    nc = SEQ // rows
    tok = lambda width: pl.BlockSpec((rows, width), lambda b, c: (b * nc + c, 0))
    return pl.pallas_call(
        _mlstm_kernel,
        grid=(BATCH, nc),
        in_specs=[tok(M_WIDTH), tok(M_WIDTH), tok(M_WIDTH), tok(M_WIDTH), tok(LANES), tok(LANES), tok(LANES),
                  pl.BlockSpec((SUBLANES, rows), lambda b, c: (0, b * nc + c))],
        out_specs=tok(M_WIDTH),
        out_shape=jax.ShapeDtypeStruct((N_TOK, M_WIDTH), BF16),
        scratch_shapes=[pltpu.VMEM((M_HEADS, M_HEAD_DIM, M_HEAD_DIM), F32),
                        pltpu.VMEM((SUBLANES, M_HEAD_DIM), F32),
                        pltpu.VMEM((SUBLANES, LANES), F32)],
        compiler_params=pltpu.CompilerParams(dimension_semantics=("arbitrary", "arbitrary"),
                                             vmem_limit_bytes=VMEM_LIMIT),
        name="mlstm",
    )(mq, mk, mv, sog, ga, gb, gm, grow)


def _swa_kernel(q_ref, kp_ref, kc_ref, vp_ref, vc_ref, sink_ref, o_ref):
    n = pl.program_id(1)
    blk = A_BLOCK
    tri = lax.broadcasted_iota(jnp.int32, (blk, blk), 0) >= lax.broadcasted_iota(jnp.int32, (blk, blk), 1)
    first_bias = jnp.where(n > 0, 0.0, -jnp.inf)
    low_kv = lax.broadcasted_iota(jnp.int32, (2 * blk, LANES), 1) < A_HEAD_DIM
    low_out = lax.broadcasted_iota(jnp.int32, (blk, LANES), 1) < A_HEAD_DIM
    ones = jnp.ones((2 * blk, LANES), BF16)
    zero_kv = jnp.zeros((2 * blk, LANES), BF16)
    for sb in range(A_STEP_BLOCKS):
        rows = slice(sb * blk, (sb + 1) * blk)
        prows = slice((sb - 1) * blk, sb * blk)
        for hk in range(A_KV_HEADS):
            kvcols = slice(hk * LANES, (hk + 1) * LANES)
            k_prev = kp_ref[:, kvcols] if sb == 0 else kc_ref[prows, kvcols]
            v_prev = vp_ref[:, kvcols] if sb == 0 else vc_ref[prows, kvcols]
            kd = jnp.concatenate([kc_ref[rows, kvcols], k_prev], axis=0)
            vd = jnp.concatenate([vc_ref[rows, kvcols], v_prev], axis=0)
            k_half = (jnp.where(low_kv, kd, zero_kv), jnp.where(low_kv, zero_kv, kd))
            v_ext = jnp.concatenate([vd, ones], axis=1)
            pairs = [hk * (A_GROUP // 2) + j for j in range(A_GROUP // 2)]
            qcat = jnp.concatenate([q_ref[rows, pr * LANES:(pr + 1) * LANES] for pr in pairs], axis=0)
            probs, sink_w = {}, {}
            for half in range(2):
                s2 = lax.dot_general(qcat, k_half[half], (((1,), (1,)), ((), ())), preferred_element_type=F32)
                for j, pr in enumerate(pairs):
                    hq = 2 * pr + half
                    sj = s2[j * blk:(j + 1) * blk]
                    s_prev = sj[:, blk:] + first_bias if sb == 0 else sj[:, blk:]
                    s = jnp.where(tri, sj[:, :blk], s_prev)
                    sink = sink_ref[hq:hq + 1, 0:1]
                    mx = jnp.maximum(jnp.broadcast_to(jnp.max(s, axis=-1, keepdims=True), (blk, blk)), sink)
                    p = jnp.exp2(s - mx)
                    sink_w[hq] = jnp.exp2(sink - mx)
                    probs[hq] = jnp.concatenate([jnp.where(tri, p, 0.0), jnp.where(tri, 0.0, p)],
                                                axis=1).astype(BF16)
            heads = sorted(probs)
            o2 = jnp.dot(jnp.concatenate([probs[hq] for hq in heads], axis=0), v_ext,
                         preferred_element_type=F32)
            outs = {}
            for r, hq in enumerate(heads):
                oh = o2[r * blk:(r + 1) * blk]
                outs[hq] = oh[:, :LANES] / (oh[:, LANES:] + sink_w[hq])
            for pr in pairs:
                o_ref[rows, pr * LANES:(pr + 1) * LANES] = jnp.where(low_out, outs[2 * pr],
                                                                      outs[2 * pr + 1]).astype(BF16)


def _swa_call(aq, ak, av, layer, sinks):
    nb = SEQ // A_BLOCK
    ns = nb // A_STEP_BLOCKS
    rows = A_STEP_BLOCKS * A_BLOCK
    cur = lambda width: pl.BlockSpec((rows, width), lambda b, n: (b * ns + n, 0))
    prev = lambda width: pl.BlockSpec((A_BLOCK, width),
                                      lambda b, n: (b * nb + jnp.maximum(n * A_STEP_BLOCKS - 1, 0), 0))
    return pl.pallas_call(
        _swa_kernel,
        grid=(BATCH, ns),
        in_specs=[cur(A_WIDTH), prev(A_KV_DUP), cur(A_KV_DUP), prev(A_KV_DUP), cur(A_KV_DUP),
                  _layer_spec(layer, (A_Q_HEADS, LANES))],
        out_specs=cur(A_WIDTH),
        out_shape=jax.ShapeDtypeStruct((N_TOK, A_WIDTH), BF16),
        compiler_params=pltpu.CompilerParams(dimension_semantics=("arbitrary", "arbitrary"),
                                             vmem_limit_bytes=VMEM_LIMIT),
        name="swa",
    )(aq, ak, ak, av, av, sinks)


def _merge_kernel(h_ref, hm_ref, ha_ref, sgm_ref, sga_ref, wm_ref, wa_ref, wo_ref, o_ref):
    bm = jnp.dot(hm_ref[...], wm_ref[...], preferred_element_type=F32)
    ba = jnp.dot(ha_ref[...], wa_ref[...], preferred_element_type=F32)
    merged = (sgm_ref[...] * bm + sga_ref[...] * ba).astype(BF16)
    o_ref[...] = h_ref[...] + jnp.dot(merged, wo_ref[...], preferred_element_type=F32)


def _merge_call(h, hm, ha, sgm, sga, layer, wm, wa, wo):
    tok = pl.BlockSpec((TM, D_MODEL), lambda i: (i, 0))
    wspec = _layer_spec(layer, (D_MODEL, D_MODEL))
    return pl.pallas_call(
        _merge_kernel,
        grid=(N_TOK // TM,),
        in_specs=[tok, tok, tok, tok, tok, wspec, wspec, wspec],
        out_specs=tok,
        out_shape=jax.ShapeDtypeStruct((N_TOK, D_MODEL), F32),
        compiler_params=pltpu.CompilerParams(dimension_semantics=("arbitrary",),
                                             vmem_limit_bytes=VMEM_LIMIT),
        name="merge",
    )(h, hm, ha, sgm, sga, wm, wa, wo)


W_IN_OFFS = {"qk": (0, 2 * M_WIDTH), "v": (2048, M_WIDTH), "o": (3072, M_WIDTH), "gm": (4096, D_MODEL),
             "ga": (5120, D_MODEL), "aq": (6144, A_WIDTH), "ak": (7168, A_KV_WIDTH), "av": (7424, A_KV_WIDTH)}


def _head_sum_matrix():
    idx = np.arange(MXU_DIM) // A_HEAD_DIM
    return jnp.asarray(idx[:, None] == idx[None, :], dtype=BF16)


def _prep(ffn1_norm, ffn1_w_gate, ffn1_w_up, ffn1_w_down, mix_norm, w_in, m_conv_w, m_conv_b,
          m_igate_b, m_fgate_b, m_out_norm, a_q_norm, a_k_norm, a_sinks, w_branch_m, w_branch_a,
          w_out, ffn2_norm, ffn2_w_gate, ffn2_w_up, ffn2_w_down, ple_norm, ple_gate_w, ple_proj_w):
    ff_pad = D_FF_PAD - D_FF
    row = lambda v: v.reshape(DEPTH, 1, -1).astype(F32)
    wide = lambda wt: jnp.pad(wt, ((0, 0), (0, 0), (0, ff_pad))).astype(BF16)
    tall = lambda wt: jnp.pad(wt, ((0, 0), (0, ff_pad), (0, 0))).astype(BF16)
    sizes = [2 * M_WIDTH, M_WIDTH, M_WIDTH, M_HEADS, M_HEADS, A_WIDTH, A_KV_WIDTH, A_KV_WIDTH, D_MODEL, D_MODEL]
    offs = np.concatenate([[0], np.cumsum(sizes)])
    part = lambda j: w_in[:, :, int(offs[j]):int(offs[j + 1])]
    w_all = jnp.concatenate([part(0), part(1), part(2), part(8), part(9), part(5), part(6), part(7)],
                            axis=-1).astype(BF16)
    lane_pad = lambda wt: jnp.pad(wt, ((0, 0), (0, 0), (0, LANES - M_HEADS)))
    sub_pad = lambda wt: jnp.pad(wt, ((0, 0), (0, SUBLANES - M_HEADS), (0, 0)))
    w_i, w_f = part(3).astype(BF16), part(4).astype(BF16)
    b_i, b_f = m_igate_b.astype(F32), m_fgate_b.astype(F32)
    return dict(
        ffn1=(row(ffn1_norm), wide(ffn1_w_gate), wide(ffn1_w_up), tall(ffn1_w_down)),
        ffn2=(row(ffn2_norm), wide(ffn2_w_gate), wide(ffn2_w_up), tall(ffn2_w_down)),
        ple=(row(ple_norm), ple_gate_w.astype(BF16), ple_proj_w.astype(BF16)),
        mix_norm=row(mix_norm), w_in=w_all,
        wif=jnp.concatenate([lane_pad(w_i), lane_pad(w_f)], axis=-1),
        wift=jnp.concatenate([sub_pad(jnp.swapaxes(w_i, 1, 2)), sub_pad(jnp.swapaxes(w_f, 1, 2))], axis=1),
        gate_b_col=jnp.concatenate([lane_pad(b_i[:, None, :]), lane_pad(b_f[:, None, :])], axis=-1),
        gate_b_row=jnp.concatenate([sub_pad(b_i[:, :, None]), sub_pad(b_f[:, :, None])], axis=1),
        conv_w=m_conv_w.astype(F32), conv_b=row(m_conv_b), head_sum=_head_sum_matrix(),
        q_gain=row(jnp.tile(a_q_norm, (1, A_Q_HEADS)) * (A_HEAD_DIM ** -0.5 * LOG2E)),
        k_gain=row(jnp.tile(a_k_norm, (1, A_KV_HEADS))),
        out_gain=row(m_out_norm),
        sinks=jnp.broadcast_to((a_sinks.astype(F32) * LOG2E)[:, :, None], (DEPTH, A_Q_HEADS, LANES)),
        wm=w_branch_m.astype(BF16), wa=w_branch_a.astype(BF16), wout=w_out.astype(BF16),
    )


def kernel(x, p, ffn1_norm, ffn1_w_gate, ffn1_w_up, ffn1_w_down, mix_norm, w_in, m_conv_w, m_conv_b, m_igate_b, m_fgate_b, m_out_norm, a_q_norm, a_k_norm, a_sinks, w_branch_m, w_branch_a, w_out, ffn2_norm, ffn2_w_gate, ffn2_w_up, ffn2_w_down, ple_norm, ple_gate_w, ple_proj_w):
    w = _prep(ffn1_norm, ffn1_w_gate, ffn1_w_up, ffn1_w_down, mix_norm, w_in, m_conv_w, m_conv_b, m_igate_b,
              m_fgate_b, m_out_norm, a_q_norm, a_k_norm, a_sinks, w_branch_m, w_branch_a, w_out, ffn2_norm,
              ffn2_w_gate, ffn2_w_up, ffn2_w_down, ple_norm, ple_gate_w, ple_proj_w)
    h = x.reshape(N_TOK, D_MODEL)
    p_tok = p.reshape(DEPTH, N_TOK, PLE_DIM)
    for i in range(DEPTH):
        h = _ffn_call(h, i, *w["ffn1"])
        mq, mk, mv, sog, ga, gb, gm, grow, aq, ak, av, sgm, sga = _inproj_call(h, i, w)
        hm = _mlstm_call(mq, mk, mv, sog, ga, gb, gm, grow)
        ha = _swa_call(aq, ak, av, i, w["sinks"])
        h = _merge_call(h, hm, ha, sgm, sga, i, w["wm"], w["wa"], w["wout"])
        pg, wpg, wpp = w["ple"]
        h = _ffn_call(h, i, *w["ffn2"], ple=(p_tok, pg, wpg, wpp))
    return h.reshape(BATCH, SEQ, D_MODEL)
```

```python
import math

import jax
import jax.numpy as jnp
import numpy as np
from jax import lax
from jax.experimental import pallas as pl
from jax.experimental.pallas import tpu as pltpu

D_MODEL = 1024
BATCH = 8
SEQ = 4096
DEPTH = 2
PLE_DIM = 256
D_FF = 2752
M_HEADS = 4
M_HEAD_DIM = 256
M_WIDTH = M_HEADS * M_HEAD_DIM
CONV_K = 4
A_Q_HEADS = 16
A_KV_HEADS = 4
A_HEAD_DIM = 64
A_GROUP = A_Q_HEADS // A_KV_HEADS
A_WIDTH = A_Q_HEADS * A_HEAD_DIM
A_KV_WIDTH = A_KV_HEADS * A_HEAD_DIM
WINDOW = 128
A_BLOCK = 128
EPS = 1e-6

N_TOK = BATCH * SEQ
LANES = 128
SUBLANES = 8
MXU_DIM = 256
D_FF_PAD = -(-D_FF // MXU_DIM) * MXU_DIM
TM = 512
M_CHUNK = 256
M_ROWS = 128
M_STEP_CHUNKS = 4
A_STEP_BLOCKS = 8
A_KV_DUP = A_KV_HEADS * LANES
VMEM_LIMIT = 56 * 1024 * 1024
MX_Q, MX_K, MX_V, MX_O, MX_WIDTH = 0, M_WIDTH, 2 * M_WIDTH, 3 * M_WIDTH, 4 * M_WIDTH
GC_A, GC_B, GC_M, GC_WIDTH = 0, LANES, 2 * LANES, 3 * LANES
AX_Q, AX_K, AX_V, AX_WIDTH = 0, A_WIDTH, A_WIDTH + A_KV_DUP, A_WIDTH + 2 * A_KV_DUP
SG_M, SG_A, SG_WIDTH = 0, D_MODEL, 2 * D_MODEL
LOG2E = math.log2(math.e)

BF16 = jnp.bfloat16
F32 = jnp.float32


def _layer_spec(layer, block, index=None):
    index = (layer,) + tuple(index if index is not None else (0,) * len(block))
    return pl.BlockSpec((None,) + tuple(block), lambda *_: index, pipeline_mode=pl.Buffered(1))


def _rms_scale(x):
    return lax.rsqrt(jnp.mean(x * x, axis=-1, keepdims=True) + EPS)


def _log_sigmoid(x):
    return jnp.minimum(x, 0.0) - jnp.log1p(jnp.exp(-jnp.abs(x)))


def _sigmoid(x):
    return 0.5 * jnp.tanh(0.5 * x) + 0.5


def _dup_heads_lanes(x):
    low = lax.broadcasted_iota(jnp.int32, (x.shape[0], LANES), 1) < A_HEAD_DIM
    out = []
    for j in range(x.shape[1] // LANES):
        pair = x[:, j * LANES:(j + 1) * LANES]
        swapped = pltpu.roll(pair, A_HEAD_DIM, 1)
        out += [jnp.where(low, pair, swapped), jnp.where(low, swapped, pair)]
    return jnp.concatenate(out, axis=1)


def _segment_scan(x, axis, seg, op, fill):
    pos = lax.broadcasted_iota(jnp.int32, x.shape, axis) % seg
    shift = 1
    while shift < seg:
        x = op(x, jnp.where(pos >= shift, pltpu.roll(x, shift, axis), fill))
        shift *= 2
    return x


def _ffn_body(x, g_ref, wg_ref, wu_ref, wd_ref):
    xn = (x * _rms_scale(x) * g_ref[...]).astype(BF16)
    y = jnp.zeros(x.shape, F32)
    for c in range(D_FF_PAD // MXU_DIM):
        cols = slice(c * MXU_DIM, (c + 1) * MXU_DIM)
        gate = jnp.dot(xn, wg_ref[:, cols], preferred_element_type=F32)
        up = jnp.dot(xn, wu_ref[:, cols], preferred_element_type=F32)
        act = (gate * jax.nn.sigmoid(gate) * up).astype(BF16)
        y = y + jnp.dot(act, wd_ref[cols, :], preferred_element_type=F32)
    return x + 0.5 * y


def _ffn_kernel(x_ref, g_ref, wg_ref, wu_ref, wd_ref, o_ref):
    o_ref[...] = _ffn_body(x_ref[...], g_ref, wg_ref, wu_ref, wd_ref)


def _ffn_ple_kernel(x_ref, g_ref, wg_ref, wu_ref, wd_ref, p_ref, pg_ref, wpg_ref, wpp_ref, o_ref):
    h = _ffn_body(x_ref[...], g_ref, wg_ref, wu_ref, wd_ref)
    hn = (h * _rms_scale(h) * pg_ref[...]).astype(BF16)
    gate = jax.nn.sigmoid(jnp.dot(hn, wpg_ref[...], preferred_element_type=F32))
    emb = jnp.dot(p_ref[...].astype(BF16), wpp_ref[...], preferred_element_type=F32)
    o_ref[...] = h + gate * emb


def _ffn_call(h, layer, g, wg, wu, wd, ple=None):
    tok = lambda w: pl.BlockSpec((TM, w), lambda i: (i, 0))
    in_specs = [tok(D_MODEL), _layer_spec(layer, (1, D_MODEL)), _layer_spec(layer, (D_MODEL, D_FF_PAD)),
                _layer_spec(layer, (D_MODEL, D_FF_PAD)), _layer_spec(layer, (D_FF_PAD, D_MODEL))]
    args = [h, g, wg, wu, wd]
    body = _ffn_kernel
    if ple is not None:
        p, pg, wpg, wpp = ple
        in_specs += [pl.BlockSpec((None, TM, PLE_DIM), lambda i: (layer, i, 0)), _layer_spec(layer, (1, D_MODEL)),
                     _layer_spec(layer, (D_MODEL, D_MODEL)), _layer_spec(layer, (PLE_DIM, D_MODEL))]
        args += [p, pg, wpg, wpp]
        body = _ffn_ple_kernel
    return pl.pallas_call(
        body,
        grid=(N_TOK // TM,),
        in_specs=in_specs,
        out_specs=tok(D_MODEL),
        out_shape=jax.ShapeDtypeStruct((N_TOK, D_MODEL), F32),
        compiler_params=pltpu.CompilerParams(dimension_semantics=("arbitrary",),
                                             vmem_limit_bytes=VMEM_LIMIT),
        name="ffn_ple" if ple is not None else "ffn",
    )(*args)


CONV_COLS = 256
CONV_ROWS = 64
ROW_SPLIT = 2
PLAIN_PARTS = (2 * M_WIDTH // CONV_COLS) // 4
PLAIN_COLS = D_MODEL // PLAIN_PARTS


def _inproj_kernel(h_ref, g_ref, wqk_ref, wv_ref, wo_ref, wif_ref, wift_ref, waq_ref, wak_ref, wav_ref,
                   wgm_ref, wga_ref, cw_ref, cb_ref, gbc_ref, gbr_ref, pq_ref, qg_ref, kg_ref, og_ref,
                   mx_ref, gcol_ref, grow_ref, ax_ref, sg_ref, *scratch):
    zbufs, stages = scratch[:-2], scratch[-2:]
    i = pl.program_id(0)

    @pl.when(i % (SEQ // TM) == 0)
    def _():
        for zb in zbufs:
            zb[0:SUBLANES, :] = jnp.zeros((SUBLANES, CONV_COLS), F32)

    x = h_ref[...]
    u = (x * _rms_scale(x) * g_ref[...]).astype(BF16)

    n_groups = 2 * M_WIDTH // CONV_COLS

    half = TM // ROW_SPLIT
    periods = [(c, hf) for c in range(n_groups) for hf in range(ROW_SPLIT)]

    def issue_dots(idx):
        c, hf = periods[idx]
        uh = u[hf * half:(hf + 1) * half]
        cols = slice(c * CONV_COLS, (c + 1) * CONV_COLS)
        zbufs[c][SUBLANES + hf * half:SUBLANES + (hf + 1) * half, :] = jnp.dot(
            uh, wqk_ref[:, cols], preferred_element_type=F32)
        which, part = divmod(c, PLAIN_PARTS)
        pc = slice(part * PLAIN_COLS, (part + 1) * PLAIN_COLS)
        stages[idx % 2][...] = jnp.dot(uh, (wv_ref, wo_ref, wgm_ref, wga_ref)[which][:, pc],
                                       preferred_element_type=F32)

    issue_dots(0)
    for idx, (c, hf) in enumerate(periods):
        if idx + 1 < len(periods):
            issue_dots(idx + 1)
        cols = slice(c * CONV_COLS, (c + 1) * CONV_COLS)
        which, part = divmod(c, PLAIN_PARTS)
        pc = slice(part * PLAIN_COLS, (part + 1) * PLAIN_COLS)
        st, zb = stages[idx % 2], zbufs[c]
        for rb in range(half // CONV_ROWS):
            row0 = hf * half + rb * CONV_ROWS
            win = zb[row0:row0 + SUBLANES + CONV_ROWS, :]
            win1 = pltpu.roll(win, 1, 0)
            near = win * cw_ref[3:4, cols] + win1 * cw_ref[2:3, cols]
            far = win * cw_ref[1:2, cols] + win1 * cw_ref[0:1, cols]
            acc = (near + pltpu.roll(far, 2, 0))[SUBLANES:] + cb_ref[:, cols]
            half_acc = 0.5 * acc
            y = half_acc * jnp.tanh(half_acc) + half_acc
            rows = slice(row0, row0 + CONV_ROWS)
            if c < M_WIDTH // CONV_COLS:
                mx_ref[rows, cols] = (y * (M_HEAD_DIM ** -0.5)).astype(BF16)
            else:
                mx_ref[rows, cols] = y.astype(BF16)
        if hf == ROW_SPLIT - 1:
            zb[0:SUBLANES, :] = zb[TM:TM + SUBLANES, :]
        for rb in range(half // CONV_ROWS):
            rows = slice(hf * half + rb * CONV_ROWS, hf * half + (rb + 1) * CONV_ROWS)
            r = st[rb * CONV_ROWS:(rb + 1) * CONV_ROWS, :]
            if which == 0:
                mx_ref[rows, slice(MX_V + pc.start, MX_V + pc.stop)] = r.astype(BF16)
            elif which == 1:
                gated = _sigmoid(r) * og_ref[:, pc]
                mx_ref[rows, slice(MX_O + pc.start, MX_O + pc.stop)] = gated.astype(BF16)
            elif which == 2:
                sg_ref[rows, slice(SG_M + pc.start, SG_M + pc.stop)] = _sigmoid(r).astype(BF16)
            else:
                sg_ref[rows, slice(SG_A + pc.start, SG_A + pc.stop)] = _sigmoid(r).astype(BF16)

    zc = jnp.dot(u, wif_ref[...], preferred_element_type=F32) + gbc_ref[...]
    b_col = _segment_scan(_log_sigmoid(zc[:, LANES:]) * LOG2E, 0, M_CHUNK, jnp.add, 0.0)
    a_col = zc[:, :LANES] * LOG2E - b_col
    gcol_ref[:, GC_A:GC_B] = a_col
    gcol_ref[:, GC_B:GC_M] = b_col
    gcol_ref[:, GC_M:GC_WIDTH] = _segment_scan(a_col, 0, M_CHUNK, jnp.maximum, -jnp.inf)
    zr = lax.dot_general(wift_ref[...], u, (((1,), (1,)), ((), ())), preferred_element_type=F32) + gbr_ref[...]
    b_row = _segment_scan(_log_sigmoid(zr[SUBLANES:]) * LOG2E, 1, M_CHUNK, jnp.add, 0.0)
    grow_ref[...] = zr[:SUBLANES] * LOG2E - b_row

    head_sum = pq_ref[...]

    def head_norm(z):
        ss = jnp.dot((z * z).astype(BF16), head_sum, preferred_element_type=F32)
        return z * lax.rsqrt(ss * (1.0 / A_HEAD_DIM) + EPS)

    zq = jnp.dot(u, waq_ref[...], preferred_element_type=F32)
    for g in range(A_WIDTH // MXU_DIM):
        gc = slice(g * MXU_DIM, (g + 1) * MXU_DIM)
        ax_ref[:, gc] = (head_norm(zq[:, gc]) * qg_ref[:, gc]).astype(BF16)
    zk = jnp.dot(u, wak_ref[...], preferred_element_type=F32)
    ax_ref[:, AX_K:AX_V] = _dup_heads_lanes(head_norm(zk) * kg_ref[...]).astype(BF16)
    zv = jnp.dot(u, wav_ref[...], preferred_element_type=F32)
    ax_ref[:, AX_V:AX_WIDTH] = _dup_heads_lanes(zv).astype(BF16)


def _inproj_call(h, layer, w):
    tok = lambda width: pl.BlockSpec((TM, width), lambda i: (i, 0))
    w_all = w["w_in"]

    def group(name):
        off, width = W_IN_OFFS[name]
        return w_all, _layer_spec(layer, (D_MODEL, width), (0, off // width))

    whole = lambda name: (w[name], _layer_spec(layer, w[name].shape[1:]))
    consts = [whole("mix_norm"), group("qk"), group("v"), group("o"), whole("wif"), whole("wift"), group("aq"),
              group("ak"), group("av"), group("gm"), group("ga"), whole("conv_w"), whole("conv_b"),
              whole("gate_b_col"), whole("gate_b_row"),
              (w["head_sum"], pl.BlockSpec((MXU_DIM, MXU_DIM), lambda i: (0, 0), pipeline_mode=pl.Buffered(1))),
              whole("q_gain"), whole("k_gain"), whole("out_gain")]
    out_shape = [
        jax.ShapeDtypeStruct((N_TOK, MX_WIDTH), BF16),
        jax.ShapeDtypeStruct((N_TOK, GC_WIDTH), F32),
        jax.ShapeDtypeStruct((SUBLANES, N_TOK), F32),
        jax.ShapeDtypeStruct((N_TOK, AX_WIDTH), BF16),
        jax.ShapeDtypeStruct((N_TOK, SG_WIDTH), BF16),
    ]
    out_specs = [tok(MX_WIDTH), tok(GC_WIDTH), pl.BlockSpec((SUBLANES, TM), lambda i: (0, i)),
                 tok(AX_WIDTH), tok(SG_WIDTH)]
    return pl.pallas_call(
        _inproj_kernel,
        grid=(N_TOK // TM,),
        in_specs=[tok(D_MODEL)] + [spec for _, spec in consts],
        out_specs=out_specs,
        out_shape=out_shape,
        scratch_shapes=([pltpu.VMEM((SUBLANES + TM, CONV_COLS), F32)] * (2 * M_WIDTH // CONV_COLS)
                        + [pltpu.VMEM((TM // ROW_SPLIT, PLAIN_COLS), F32)] * 2),
        compiler_params=pltpu.CompilerParams(dimension_semantics=("arbitrary",),
                                             vmem_limit_bytes=VMEM_LIMIT),
        name="inproj",
    )(h, *[arr for arr, _ in consts])


def _mlstm_kernel(mx_ref, gcol_ref, grow_ref, o_ref, c_ref, n_ref, m_ref):
    L, R = M_CHUNK, M_ROWS

    @pl.when(pl.program_id(1) == 0)
    def _():
        c_ref[...] = jnp.zeros(c_ref.shape, F32)
        n_ref[...] = jnp.zeros(n_ref.shape, F32)
        m_ref[...] = jnp.zeros(m_ref.shape, F32)

    tri = lax.broadcasted_iota(jnp.int32, (R, R), 0) >= lax.broadcasted_iota(jnp.int32, (R, R), 1)
    lane = lax.broadcasted_iota(jnp.int32, (L, LANES), 1)
    for ch in range(M_STEP_CHUNKS):
        _mlstm_chunk(ch * L, tri, lane, mx_ref, gcol_ref, grow_ref, o_ref, c_ref, n_ref, m_ref)


def _mlstm_chunk(base, tri, lane, mx_ref, gcol_ref, grow_ref, o_ref, c_ref, n_ref, m_ref):
    L, R = M_CHUNK, M_ROWS
    chunk = slice(base, base + L)
    ga, gb, gm = gcol_ref[chunk, GC_A:GC_B], gcol_ref[chunk, GC_B:GC_M], gcol_ref[chunk, GC_M:GC_WIDTH]
    for hd in range(M_HEADS):
        cols = slice(hd * M_HEAD_DIM, (hd + 1) * M_HEAD_DIM)
        qcols, kcols, vcols, ocols = (slice(off + cols.start, off + cols.stop) for off in (MX_Q, MX_K, MX_V, MX_O))
        a_row = grow_ref[hd:hd + 1, chunk]
        pick = lane == hd
        a_col = jnp.sum(jnp.where(pick, ga, 0.0), axis=-1, keepdims=True)
        b_col = jnp.sum(jnp.where(pick, gb, 0.0), axis=-1, keepdims=True)
        cm_col = jnp.sum(jnp.where(pick, gm, 0.0), axis=-1, keepdims=True)
        m_prev = jnp.max(m_ref[hd:hd + 1, :], axis=-1, keepdims=True)
        n_prev = n_ref[hd:hd + 1, :]
        ck_prev = c_ref[hd]
        ck_bf = ck_prev.astype(BF16)
        m_loc = jnp.maximum(cm_col, m_prev)
        w_inter = jnp.exp2(m_prev - m_loc)
        floor = jnp.exp2(-(b_col + m_loc))
        k = mx_ref[chunk, kcols]
        v = mx_ref[chunk, vcols]
        for rb in range(L // R):
            rows = slice(rb * R, (rb + 1) * R)
            orows = slice(base + rb * R, base + (rb + 1) * R)
            width = (rb + 1) * R
            q = mx_ref[orows, qcols]
            s = lax.dot_general(q, k[:width], (((1,), (1,)), ((), ())), preferred_element_type=F32)
            ml = m_loc[rows]
            wi = w_inter[rows]
            blocks = []
            for cb in range(rb + 1):
                e = jnp.exp2(a_row[:, cb * R:(cb + 1) * R] - ml)
                if cb == rb:
                    e = jnp.where(tri, e, 0.0)
                blocks.append(s[:, cb * R:(cb + 1) * R] * e)
            sw = blocks[0] if rb == 0 else jnp.concatenate(blocks, axis=1)
            num = (jnp.dot(sw.astype(BF16), v[:width], preferred_element_type=F32)
                   + wi * jnp.dot(q, ck_bf, preferred_element_type=F32))
            qn = q.astype(F32) * n_prev
            dsum = wi * (qn[:, :LANES] + qn[:, LANES:])
            for blk in blocks:
                dsum = dsum + blk
            den = jnp.sum(dsum, axis=-1, keepdims=True)
            dd = jnp.maximum(jnp.abs(den), floor[rows])
            scale = lax.rsqrt(jnp.mean(num * num, axis=-1, keepdims=True) + EPS * (dd * dd))
            o_ref[orows, cols] = (num * scale * mx_ref[orows, ocols]).astype(BF16)

        m_last = m_loc[L - 1:L, :]
        kw = k.astype(F32) * jnp.exp2(a_col - m_last)
        decay = jnp.exp2(m_prev - m_last)
        c_ref[hd] = decay * ck_prev + lax.dot_general(kw.astype(BF16), v, (((0,), (0,)), ((), ())),
                                                      preferred_element_type=F32)
        n_ref[hd:hd + 1, :] = decay * n_prev + jnp.sum(kw, axis=0, keepdims=True)
        m_ref[hd:hd + 1, :] = jnp.broadcast_to(b_col[L - 1:L, :] + m_last, (1, LANES))


def _mlstm_call(mx, gcol, grow):
    rows = M_STEP_CHUNKS * M_CHUNK
    nc = SEQ // rows
    tok = lambda width: pl.BlockSpec((rows, width), lambda b, c: (b * nc + c, 0))
    return pl.pallas_call(
        _mlstm_kernel,
        grid=(BATCH, nc),
        in_specs=[tok(MX_WIDTH), tok(GC_WIDTH), pl.BlockSpec((SUBLANES, rows), lambda b, c: (0, b * nc + c))],
        out_specs=tok(M_WIDTH),
        out_shape=jax.ShapeDtypeStruct((N_TOK, M_WIDTH), BF16),
        scratch_shapes=[pltpu.VMEM((M_HEADS, M_HEAD_DIM, M_HEAD_DIM), F32),
                        pltpu.VMEM((SUBLANES, M_HEAD_DIM), F32),
                        pltpu.VMEM((SUBLANES, LANES), F32)],
        compiler_params=pltpu.CompilerParams(dimension_semantics=("arbitrary", "arbitrary"),
                                             vmem_limit_bytes=VMEM_LIMIT),
        name="mlstm",
    )(mx, gcol, grow)


def _swa_kernel(ax_ref, prev_ref, sink_ref, o_ref):
    n = pl.program_id(1)
    blk = A_BLOCK
    tri = lax.broadcasted_iota(jnp.int32, (blk, blk), 0) >= lax.broadcasted_iota(jnp.int32, (blk, blk), 1)
    first_bias = jnp.where(n > 0, 0.0, -jnp.inf)
    low_kv = lax.broadcasted_iota(jnp.int32, (2 * blk, LANES), 1) < A_HEAD_DIM
    low_out = lax.broadcasted_iota(jnp.int32, (blk, LANES), 1) < A_HEAD_DIM
    ones = jnp.ones((2 * blk, LANES), BF16)
    zero_kv = jnp.zeros((2 * blk, LANES), BF16)
    for sb in range(A_STEP_BLOCKS):
        rows = slice(sb * blk, (sb + 1) * blk)
        prows = slice((sb - 1) * blk, sb * blk)
        for hk in range(A_KV_HEADS):
            kcols = slice(AX_K + hk * LANES, AX_K + (hk + 1) * LANES)
            vcols = slice(AX_V + hk * LANES, AX_V + (hk + 1) * LANES)
            k_prev = prev_ref[:, slice(kcols.start - AX_K, kcols.stop - AX_K)] if sb == 0 else ax_ref[prows, kcols]
            v_prev = prev_ref[:, slice(vcols.start - AX_K, vcols.stop - AX_K)] if sb == 0 else ax_ref[prows, vcols]
            kd = jnp.concatenate([ax_ref[rows, kcols], k_prev], axis=0)
            vd = jnp.concatenate([ax_ref[rows, vcols], v_prev], axis=0)
            k_half = (jnp.where(low_kv, kd, zero_kv), jnp.where(low_kv, zero_kv, kd))
            v_ext = jnp.concatenate([vd, ones], axis=1)
            pairs = [hk * (A_GROUP // 2) + j for j in range(A_GROUP // 2)]
            qcat = jnp.concatenate([ax_ref[rows, pr * LANES:(pr + 1) * LANES] for pr in pairs], axis=0)
            probs, sink_w = {}, {}
            for half in range(2):
                s2 = lax.dot_general(qcat, k_half[half], (((1,), (1,)), ((), ())), preferred_element_type=F32)
                for j, pr in enumerate(pairs):
                    hq = 2 * pr + half
                    sj = s2[j * blk:(j + 1) * blk]
                    s_prev = sj[:, blk:] + first_bias if sb == 0 else sj[:, blk:]
                    s = jnp.where(tri, sj[:, :blk], s_prev)
                    sink = sink_ref[hq:hq + 1, 0:1]
                    mx = jnp.maximum(jnp.broadcast_to(jnp.max(s, axis=-1, keepdims=True), (blk, blk)), sink)
                    p = jnp.exp2(s - mx)
                    sink_w[hq] = jnp.exp2(sink - mx)
                    probs[hq] = jnp.concatenate([jnp.where(tri, p, 0.0), jnp.where(tri, 0.0, p)],
                                                axis=1).astype(BF16)
            heads = sorted(probs)
            o2 = jnp.dot(jnp.concatenate([probs[hq] for hq in heads], axis=0), v_ext,
                         preferred_element_type=F32)
            outs = {}
            for r, hq in enumerate(heads):
                oh = o2[r * blk:(r + 1) * blk]
                outs[hq] = oh[:, :LANES] / (oh[:, LANES:] + sink_w[hq])
            for pr in pairs:
                o_ref[rows, pr * LANES:(pr + 1) * LANES] = jnp.where(low_out, outs[2 * pr],
                                                                      outs[2 * pr + 1]).astype(BF16)


def _swa_call(ax, layer, sinks):
    nb = SEQ // A_BLOCK
    ns = nb // A_STEP_BLOCKS
    rows = A_STEP_BLOCKS * A_BLOCK
    cur = lambda width: pl.BlockSpec((rows, width), lambda b, n: (b * ns + n, 0))
    prev = pl.BlockSpec((A_BLOCK, AX_WIDTH - AX_K),
                        lambda b, n: (b * nb + jnp.maximum(n * A_STEP_BLOCKS - 1, 0), AX_K // (AX_WIDTH - AX_K)))
    return pl.pallas_call(
        _swa_kernel,
        grid=(BATCH, ns),
        in_specs=[cur(AX_WIDTH), prev, _layer_spec(layer, (A_Q_HEADS, LANES))],
        out_specs=cur(A_WIDTH),
        out_shape=jax.ShapeDtypeStruct((N_TOK, A_WIDTH), BF16),
        compiler_params=pltpu.CompilerParams(dimension_semantics=("arbitrary", "arbitrary"),
                                             vmem_limit_bytes=VMEM_LIMIT),
        name="swa",
    )(ax, ax, sinks)


def _merge_kernel(h_ref, hm_ref, ha_ref, sg_ref, wm_ref, wa_ref, wo_ref, o_ref):
    bm = jnp.dot(hm_ref[...], wm_ref[...], preferred_element_type=F32)
    ba = jnp.dot(ha_ref[...], wa_ref[...], preferred_element_type=F32)
    merged = (sg_ref[:, SG_M:SG_A] * bm + sg_ref[:, SG_A:SG_WIDTH] * ba).astype(BF16)
    o_ref[...] = h_ref[...] + jnp.dot(merged, wo_ref[...], preferred_element_type=F32)


def _merge_call(h, hm, ha, sg, layer, wm, wa, wo):
    tok = pl.BlockSpec((TM, D_MODEL), lambda i: (i, 0))
    wspec = _layer_spec(layer, (D_MODEL, D_MODEL))
    return pl.pallas_call(
        _merge_kernel,
        grid=(N_TOK // TM,),
        in_specs=[tok, tok, tok, pl.BlockSpec((TM, SG_WIDTH), lambda i: (i, 0)), wspec, wspec, wspec],
        out_specs=tok,
        out_shape=jax.ShapeDtypeStruct((N_TOK, D_MODEL), F32),
        compiler_params=pltpu.CompilerParams(dimension_semantics=("arbitrary",),
                                             vmem_limit_bytes=VMEM_LIMIT),
        name="merge",
    )(h, hm, ha, sg, wm, wa, wo)


W_IN_OFFS = {"qk": (0, 2 * M_WIDTH), "v": (2048, M_WIDTH), "o": (3072, M_WIDTH), "gm": (4096, D_MODEL),
             "ga": (5120, D_MODEL), "aq": (6144, A_WIDTH), "ak": (7168, A_KV_WIDTH), "av": (7424, A_KV_WIDTH)}


def _head_sum_matrix():
    idx = np.arange(MXU_DIM) // A_HEAD_DIM
    return jnp.asarray(idx[:, None] == idx[None, :], dtype=BF16)


def _prep(ffn1_norm, ffn1_w_gate, ffn1_w_up, ffn1_w_down, mix_norm, w_in, m_conv_w, m_conv_b,
          m_igate_b, m_fgate_b, m_out_norm, a_q_norm, a_k_norm, a_sinks, w_branch_m, w_branch_a,
          w_out, ffn2_norm, ffn2_w_gate, ffn2_w_up, ffn2_w_down, ple_norm, ple_gate_w, ple_proj_w):
    ff_pad = D_FF_PAD - D_FF
    row = lambda v: v.reshape(DEPTH, 1, -1).astype(F32)
    wide = lambda wt: jnp.pad(wt, ((0, 0), (0, 0), (0, ff_pad))).astype(BF16)
    tall = lambda wt: jnp.pad(wt, ((0, 0), (0, ff_pad), (0, 0))).astype(BF16)
    sizes = [2 * M_WIDTH, M_WIDTH, M_WIDTH, M_HEADS, M_HEADS, A_WIDTH, A_KV_WIDTH, A_KV_WIDTH, D_MODEL, D_MODEL]
    offs = np.concatenate([[0], np.cumsum(sizes)])
    part = lambda j: w_in[:, :, int(offs[j]):int(offs[j + 1])]
    w_all = jnp.concatenate([part(0), part(1), part(2), part(8), part(9), part(5), part(6), part(7)],
                            axis=-1).astype(BF16)
    lane_pad = lambda wt: jnp.pad(wt, ((0, 0), (0, 0), (0, LANES - M_HEADS)))
    sub_pad = lambda wt: jnp.pad(wt, ((0, 0), (0, SUBLANES - M_HEADS), (0, 0)))
    w_i, w_f = part(3).astype(BF16), part(4).astype(BF16)
    b_i, b_f = m_igate_b.astype(F32), m_fgate_b.astype(F32)
    return dict(
        ffn1=(row(ffn1_norm), wide(ffn1_w_gate), wide(ffn1_w_up), tall(ffn1_w_down)),
        ffn2=(row(ffn2_norm), wide(ffn2_w_gate), wide(ffn2_w_up), tall(ffn2_w_down)),
        ple=(row(ple_norm), ple_gate_w.astype(BF16), ple_proj_w.astype(BF16)),
        mix_norm=row(mix_norm), w_in=w_all,
        wif=jnp.concatenate([lane_pad(w_i), lane_pad(w_f)], axis=-1),
        wift=jnp.concatenate([sub_pad(jnp.swapaxes(w_i, 1, 2)), sub_pad(jnp.swapaxes(w_f, 1, 2))], axis=1),
        gate_b_col=jnp.concatenate([lane_pad(b_i[:, None, :]), lane_pad(b_f[:, None, :])], axis=-1),
        gate_b_row=jnp.concatenate([sub_pad(b_i[:, :, None]), sub_pad(b_f[:, :, None])], axis=1),
        conv_w=m_conv_w.astype(F32), conv_b=row(m_conv_b), head_sum=_head_sum_matrix(),
        q_gain=row(jnp.tile(a_q_norm, (1, A_Q_HEADS)) * (A_HEAD_DIM ** -0.5 * LOG2E)),
        k_gain=row(jnp.tile(a_k_norm, (1, A_KV_HEADS))),
        out_gain=row(m_out_norm),
        sinks=jnp.broadcast_to((a_sinks.astype(F32) * LOG2E)[:, :, None], (DEPTH, A_Q_HEADS, LANES)),
        wm=w_branch_m.astype(BF16), wa=w_branch_a.astype(BF16), wout=w_out.astype(BF16),
    )


def kernel(x, p, ffn1_norm, ffn1_w_gate, ffn1_w_up, ffn1_w_down, mix_norm, w_in, m_conv_w, m_conv_b, m_igate_b, m_fgate_b, m_out_norm, a_q_norm, a_k_norm, a_sinks, w_branch_m, w_branch_a, w_out, ffn2_norm, ffn2_w_gate, ffn2_w_up, ffn2_w_down, ple_norm, ple_gate_w, ple_proj_w):
    w = _prep(ffn1_norm, ffn1_w_gate, ffn1_w_up, ffn1_w_down, mix_norm, w_in, m_conv_w, m_conv_b, m_igate_b,
              m_fgate_b, m_out_norm, a_q_norm, a_k_norm, a_sinks, w_branch_m, w_branch_a, w_out, ffn2_norm,
              ffn2_w_gate, ffn2_w_up, ffn2_w_down, ple_norm, ple_gate_w, ple_proj_w)
    h = x.reshape(N_TOK, D_MODEL)
    p_tok = p.reshape(DEPTH, N_TOK, PLE_DIM)
    for i in range(DEPTH):
        h = _ffn_call(h, i, *w["ffn1"])
        mx, gcol, grow, ax, sg = _inproj_call(h, i, w)
        hm = _mlstm_call(mx, gcol, grow)
        ha = _swa_call(ax, i, w["sinks"])
        h = _merge_call(h, hm, ha, sg, i, w["wm"], w["wa"], w["wout"])
        pg, wpg, wpp = w["ple"]
        h = _ffn_call(h, i, *w["ffn2"], ple=(p_tok, pg, wpg, wpp))
    return h.reshape(BATCH, SEQ, D_MODEL)
```

```python
import math

import jax
import jax.numpy as jnp
import numpy as np
from jax import lax
from jax.experimental import pallas as pl
from jax.experimental.pallas import tpu as pltpu

D_MODEL = 1024
BATCH = 8
SEQ = 4096
DEPTH = 2
PLE_DIM = 256
D_FF = 2752
M_HEADS = 4
M_HEAD_DIM = 256
M_WIDTH = M_HEADS * M_HEAD_DIM
CONV_K = 4
A_Q_HEADS = 16
A_KV_HEADS = 4
A_HEAD_DIM = 64
A_GROUP = A_Q_HEADS // A_KV_HEADS
A_WIDTH = A_Q_HEADS * A_HEAD_DIM
A_KV_WIDTH = A_KV_HEADS * A_HEAD_DIM
WINDOW = 128
A_BLOCK = 128
EPS = 1e-6

N_TOK = BATCH * SEQ
LANES = 128
SUBLANES = 8
MXU_DIM = 256
D_FF_PAD = -(-D_FF // MXU_DIM) * MXU_DIM
TM = 512
M_CHUNK = 256
M_ROWS = 128
M_STEP_CHUNKS = 4
A_STEP_BLOCKS = 8
A_KV_DUP = A_KV_HEADS * LANES
VMEM_LIMIT = 56 * 1024 * 1024
MX_Q, MX_K, MX_V, MX_O, MX_WIDTH = 0, M_WIDTH, 2 * M_WIDTH, 3 * M_WIDTH, 4 * M_WIDTH
GC_A, GC_B, GC_M, GC_WIDTH = 0, LANES, 2 * LANES, 3 * LANES
AX_Q, AX_K, AX_V, AX_WIDTH = 0, A_WIDTH, A_WIDTH + A_KV_DUP, A_WIDTH + 2 * A_KV_DUP
LOG2E = math.log2(math.e)

BF16 = jnp.bfloat16
F32 = jnp.float32


def _layer_spec(layer, block, index=None):
    index = (layer,) + tuple(index if index is not None else (0,) * len(block))
    return pl.BlockSpec((None,) + tuple(block), lambda *_: index, pipeline_mode=pl.Buffered(1))


def _rms_scale(x):
    return lax.rsqrt(jnp.mean(x * x, axis=-1, keepdims=True) + EPS)


def _log_sigmoid(x):
    return jnp.minimum(x, 0.0) - jnp.log1p(jnp.exp(-jnp.abs(x)))


def _sigmoid(x):
    return 0.5 * jnp.tanh(0.5 * x) + 0.5


def _dup_heads_lanes(x):
    low = lax.broadcasted_iota(jnp.int32, (x.shape[0], LANES), 1) < A_HEAD_DIM
    out = []
    for j in range(x.shape[1] // LANES):
        pair = x[:, j * LANES:(j + 1) * LANES]
        swapped = pltpu.roll(pair, A_HEAD_DIM, 1)
        out += [jnp.where(low, pair, swapped), jnp.where(low, swapped, pair)]
    return jnp.concatenate(out, axis=1)


def _segment_scan(x, axis, seg, op, fill):
    pos = lax.broadcasted_iota(jnp.int32, x.shape, axis) % seg
    shift = 1
    while shift < seg:
        x = op(x, jnp.where(pos >= shift, pltpu.roll(x, shift, axis), fill))
        shift *= 2
    return x


def _ffn_body(x, g_ref, wg_ref, wu_ref, wd_ref):
    xn = (x * _rms_scale(x) * g_ref[...]).astype(BF16)
    y = jnp.zeros(x.shape, F32)
    for c in range(D_FF_PAD // MXU_DIM):
        cols = slice(c * MXU_DIM, (c + 1) * MXU_DIM)
        gate = jnp.dot(xn, wg_ref[:, cols], preferred_element_type=F32)
        up = jnp.dot(xn, wu_ref[:, cols], preferred_element_type=F32)
        act = (gate * jax.nn.sigmoid(gate) * up).astype(BF16)
        y = y + jnp.dot(act, wd_ref[cols, :], preferred_element_type=F32)
    return x + 0.5 * y


def _ffn_kernel(x_ref, g_ref, wg_ref, wu_ref, wd_ref, o_ref):
    o_ref[...] = _ffn_body(x_ref[...], g_ref, wg_ref, wu_ref, wd_ref)


def _ffn_ple_kernel(x_ref, g_ref, wg_ref, wu_ref, wd_ref, p_ref, pg_ref, wpg_ref, wpp_ref, o_ref):
    h = _ffn_body(x_ref[...], g_ref, wg_ref, wu_ref, wd_ref)
    hn = (h * _rms_scale(h) * pg_ref[...]).astype(BF16)
    gate = jax.nn.sigmoid(jnp.dot(hn, wpg_ref[...], preferred_element_type=F32))
    emb = jnp.dot(p_ref[...].astype(BF16), wpp_ref[...], preferred_element_type=F32)
    o_ref[...] = h + gate * emb


def _ffn_call(h, layer, g, wg, wu, wd, ple=None):
    tok = lambda w: pl.BlockSpec((TM, w), lambda i: (i, 0))
    in_specs = [tok(D_MODEL), _layer_spec(layer, (1, D_MODEL)), _layer_spec(layer, (D_MODEL, D_FF_PAD)),
                _layer_spec(layer, (D_MODEL, D_FF_PAD)), _layer_spec(layer, (D_FF_PAD, D_MODEL))]
    args = [h, g, wg, wu, wd]
    body = _ffn_kernel
    if ple is not None:
        p, pg, wpg, wpp = ple
        in_specs += [pl.BlockSpec((None, TM, PLE_DIM), lambda i: (layer, i, 0)), _layer_spec(layer, (1, D_MODEL)),
                     _layer_spec(layer, (D_MODEL, D_MODEL)), _layer_spec(layer, (PLE_DIM, D_MODEL))]
        args += [p, pg, wpg, wpp]
        body = _ffn_ple_kernel
    return pl.pallas_call(
        body,
        grid=(N_TOK // TM,),
        in_specs=in_specs,
        out_specs=tok(D_MODEL),
        out_shape=jax.ShapeDtypeStruct((N_TOK, D_MODEL), F32),
        compiler_params=pltpu.CompilerParams(dimension_semantics=("arbitrary",),
                                             vmem_limit_bytes=VMEM_LIMIT),
        name="ffn_ple" if ple is not None else "ffn",
    )(*args)


CONV_COLS = 256
CONV_ROWS = 64
ROW_SPLIT = 2
PLAIN_PARTS = (2 * M_WIDTH // CONV_COLS) // 2
PLAIN_COLS = D_MODEL // PLAIN_PARTS


def _inproj_kernel(h_ref, g_ref, wqk_ref, wv_ref, wo_ref, wif_ref, wift_ref, waq_ref, wak_ref, wav_ref,
                   cw_ref, cb_ref, gbc_ref, gbr_ref, pq_ref, qg_ref, kg_ref, og_ref,
                   mx_ref, gcol_ref, grow_ref, ax_ref, *scratch):
    zbufs, stages = scratch[:-2], scratch[-2:]
    i = pl.program_id(0)

    @pl.when(i % (SEQ // TM) == 0)
    def _():
        for zb in zbufs:
            zb[0:SUBLANES, :] = jnp.zeros((SUBLANES, CONV_COLS), F32)

    x = h_ref[...]
    u = (x * _rms_scale(x) * g_ref[...]).astype(BF16)

    n_groups = 2 * M_WIDTH // CONV_COLS

    half = TM // ROW_SPLIT
    periods = [(c, hf) for c in range(n_groups) for hf in range(ROW_SPLIT)]

    def issue_dots(idx):
        c, hf = periods[idx]
        uh = u[hf * half:(hf + 1) * half]
        cols = slice(c * CONV_COLS, (c + 1) * CONV_COLS)
        zbufs[c][SUBLANES + hf * half:SUBLANES + (hf + 1) * half, :] = jnp.dot(
            uh, wqk_ref[:, cols], preferred_element_type=F32)
        which, part = divmod(c, PLAIN_PARTS)
        pc = slice(part * PLAIN_COLS, (part + 1) * PLAIN_COLS)
        stages[idx % 2][...] = jnp.dot(uh, (wv_ref, wo_ref)[which][:, pc], preferred_element_type=F32)

    issue_dots(0)
    for idx, (c, hf) in enumerate(periods):
        if idx + 1 < len(periods):
            issue_dots(idx + 1)
        cols = slice(c * CONV_COLS, (c + 1) * CONV_COLS)
        which, part = divmod(c, PLAIN_PARTS)
        pc = slice(part * PLAIN_COLS, (part + 1) * PLAIN_COLS)
        st, zb = stages[idx % 2], zbufs[c]
        for rb in range(half // CONV_ROWS):
            row0 = hf * half + rb * CONV_ROWS
            win = zb[row0:row0 + SUBLANES + CONV_ROWS, :]
            win1 = pltpu.roll(win, 1, 0)
            near = win * cw_ref[3:4, cols] + win1 * cw_ref[2:3, cols]
            far = win * cw_ref[1:2, cols] + win1 * cw_ref[0:1, cols]
            acc = (near + pltpu.roll(far, 2, 0))[SUBLANES:] + cb_ref[:, cols]
            half_acc = 0.5 * acc
            y = half_acc * jnp.tanh(half_acc) + half_acc
            rows = slice(row0, row0 + CONV_ROWS)
            if c < M_WIDTH // CONV_COLS:
                mx_ref[rows, cols] = (y * (M_HEAD_DIM ** -0.5)).astype(BF16)
            else:
                mx_ref[rows, cols] = y.astype(BF16)
        if hf == ROW_SPLIT - 1:
            zb[0:SUBLANES, :] = zb[TM:TM + SUBLANES, :]
        for rb in range(half // CONV_ROWS):
            rows = slice(hf * half + rb * CONV_ROWS, hf * half + (rb + 1) * CONV_ROWS)
            r = st[rb * CONV_ROWS:(rb + 1) * CONV_ROWS, :]
            if which == 0:
                mx_ref[rows, slice(MX_V + pc.start, MX_V + pc.stop)] = r.astype(BF16)
            else:
                gated = _sigmoid(r) * og_ref[:, pc]
                mx_ref[rows, slice(MX_O + pc.start, MX_O + pc.stop)] = gated.astype(BF16)

    zc = jnp.dot(u, wif_ref[...], preferred_element_type=F32) + gbc_ref[...]
    b_col = _segment_scan(_log_sigmoid(zc[:, LANES:]) * LOG2E, 0, M_CHUNK, jnp.add, 0.0)
    a_col = zc[:, :LANES] * LOG2E - b_col
    gcol_ref[:, GC_A:GC_B] = a_col
    gcol_ref[:, GC_B:GC_M] = b_col
    gcol_ref[:, GC_M:GC_WIDTH] = _segment_scan(a_col, 0, M_CHUNK, jnp.maximum, -jnp.inf)
    zr = lax.dot_general(wift_ref[...], u, (((1,), (1,)), ((), ())), preferred_element_type=F32) + gbr_ref[...]
    b_row = _segment_scan(_log_sigmoid(zr[SUBLANES:]) * LOG2E, 1, M_CHUNK, jnp.add, 0.0)
    grow_ref[...] = zr[:SUBLANES] * LOG2E - b_row

    head_sum = pq_ref[...]

    def head_norm(z):
        ss = jnp.dot((z * z).astype(BF16), head_sum, preferred_element_type=F32)
        return z * lax.rsqrt(ss * (1.0 / A_HEAD_DIM) + EPS)

    zq = jnp.dot(u, waq_ref[...], preferred_element_type=F32)
    for g in range(A_WIDTH // MXU_DIM):
        gc = slice(g * MXU_DIM, (g + 1) * MXU_DIM)
        ax_ref[:, gc] = (head_norm(zq[:, gc]) * qg_ref[:, gc]).astype(BF16)
    zk = jnp.dot(u, wak_ref[...], preferred_element_type=F32)
    ax_ref[:, AX_K:AX_V] = _dup_heads_lanes(head_norm(zk) * kg_ref[...]).astype(BF16)
    zv = jnp.dot(u, wav_ref[...], preferred_element_type=F32)
    ax_ref[:, AX_V:AX_WIDTH] = _dup_heads_lanes(zv).astype(BF16)


def _inproj_call(h, layer, w):
    tok = lambda width: pl.BlockSpec((TM, width), lambda i: (i, 0))
    w_all = w["w_in"]

    def group(name):
        off, width = W_IN_OFFS[name]
        return w_all, _layer_spec(layer, (D_MODEL, width), (0, off // width))

    whole = lambda name: (w[name], _layer_spec(layer, w[name].shape[1:]))
    consts = [whole("mix_norm"), group("qk"), group("v"), group("o"), whole("wif"), whole("wift"), group("aq"),
              group("ak"), group("av"), whole("conv_w"), whole("conv_b"),
              whole("gate_b_col"), whole("gate_b_row"),
              (w["head_sum"], pl.BlockSpec((MXU_DIM, MXU_DIM), lambda i: (0, 0), pipeline_mode=pl.Buffered(1))),
              whole("q_gain"), whole("k_gain"), whole("out_gain")]
    out_shape = [
        jax.ShapeDtypeStruct((N_TOK, MX_WIDTH), BF16),
        jax.ShapeDtypeStruct((N_TOK, GC_WIDTH), F32),
        jax.ShapeDtypeStruct((SUBLANES, N_TOK), F32),
        jax.ShapeDtypeStruct((N_TOK, AX_WIDTH), BF16),
    ]
    out_specs = [tok(MX_WIDTH), tok(GC_WIDTH), pl.BlockSpec((SUBLANES, TM), lambda i: (0, i)),
                 tok(AX_WIDTH)]
    return pl.pallas_call(
        _inproj_kernel,
        grid=(N_TOK // TM,),
        in_specs=[tok(D_MODEL)] + [spec for _, spec in consts],
        out_specs=out_specs,
        out_shape=out_shape,
        scratch_shapes=([pltpu.VMEM((SUBLANES + TM, CONV_COLS), F32)] * (2 * M_WIDTH // CONV_COLS)
                        + [pltpu.VMEM((TM // ROW_SPLIT, PLAIN_COLS), F32)] * 2),
        compiler_params=pltpu.CompilerParams(dimension_semantics=("arbitrary",),
                                             vmem_limit_bytes=VMEM_LIMIT),
        name="inproj",
    )(h, *[arr for arr, _ in consts])


def _mlstm_kernel(mx_ref, gcol_ref, grow_ref, o_ref, c_ref, n_ref, m_ref):
    L, R = M_CHUNK, M_ROWS

    @pl.when(pl.program_id(1) == 0)
    def _():
        c_ref[...] = jnp.zeros(c_ref.shape, F32)
        n_ref[...] = jnp.zeros(n_ref.shape, F32)
        m_ref[...] = jnp.zeros(m_ref.shape, F32)

    tri = lax.broadcasted_iota(jnp.int32, (R, R), 0) >= lax.broadcasted_iota(jnp.int32, (R, R), 1)
    lane = lax.broadcasted_iota(jnp.int32, (L, LANES), 1)
    for ch in range(M_STEP_CHUNKS):
        _mlstm_chunk(ch * L, tri, lane, mx_ref, gcol_ref, grow_ref, o_ref, c_ref, n_ref, m_ref)


def _mlstm_chunk(base, tri, lane, mx_ref, gcol_ref, grow_ref, o_ref, c_ref, n_ref, m_ref):
    L, R = M_CHUNK, M_ROWS
    chunk = slice(base, base + L)
    ga, gb, gm = gcol_ref[chunk, GC_A:GC_B], gcol_ref[chunk, GC_B:GC_M], gcol_ref[chunk, GC_M:GC_WIDTH]
    for hd in range(M_HEADS):
        cols = slice(hd * M_HEAD_DIM, (hd + 1) * M_HEAD_DIM)
        qcols, kcols, vcols, ocols = (slice(off + cols.start, off + cols.stop) for off in (MX_Q, MX_K, MX_V, MX_O))
        a_row = grow_ref[hd:hd + 1, chunk]
        pick = lane == hd
        a_col = jnp.sum(jnp.where(pick, ga, 0.0), axis=-1, keepdims=True)
        b_col = jnp.sum(jnp.where(pick, gb, 0.0), axis=-1, keepdims=True)
        cm_col = jnp.sum(jnp.where(pick, gm, 0.0), axis=-1, keepdims=True)
        m_prev = jnp.max(m_ref[hd:hd + 1, :], axis=-1, keepdims=True)
        n_prev = n_ref[hd:hd + 1, :]
        ck_prev = c_ref[hd]
        ck_bf = ck_prev.astype(BF16)
        m_loc = jnp.maximum(cm_col, m_prev)
        w_inter = jnp.exp2(m_prev - m_loc)
        floor = jnp.exp2(-(b_col + m_loc))
        k = mx_ref[chunk, kcols]
        v = mx_ref[chunk, vcols]
        for rb in range(L // R):
            rows = slice(rb * R, (rb + 1) * R)
            orows = slice(base + rb * R, base + (rb + 1) * R)
            width = (rb + 1) * R
            q = mx_ref[orows, qcols]
            s = lax.dot_general(q, k[:width], (((1,), (1,)), ((), ())), preferred_element_type=F32)
            ml = m_loc[rows]
            wi = w_inter[rows]
            blocks = []
            for cb in range(rb + 1):
                e = jnp.exp2(a_row[:, cb * R:(cb + 1) * R] - ml)
                if cb == rb:
                    e = jnp.where(tri, e, 0.0)
                blocks.append(s[:, cb * R:(cb + 1) * R] * e)
            sw = blocks[0] if rb == 0 else jnp.concatenate(blocks, axis=1)
            num = (jnp.dot(sw.astype(BF16), v[:width], preferred_element_type=F32)
                   + wi * jnp.dot(q, ck_bf, preferred_element_type=F32))
            qn = q.astype(F32) * n_prev
            dsum = wi * (qn[:, :LANES] + qn[:, LANES:])
            for blk in blocks:
                dsum = dsum + blk
            den = jnp.sum(dsum, axis=-1, keepdims=True)
            dd = jnp.maximum(jnp.abs(den), floor[rows])
            scale = lax.rsqrt(jnp.mean(num * num, axis=-1, keepdims=True) + EPS * (dd * dd))
            o_ref[orows, cols] = (num * scale * mx_ref[orows, ocols]).astype(BF16)

        m_last = m_loc[L - 1:L, :]
        kw = k.astype(F32) * jnp.exp2(a_col - m_last)
        decay = jnp.exp2(m_prev - m_last)
        c_ref[hd] = decay * ck_prev + lax.dot_general(kw.astype(BF16), v, (((0,), (0,)), ((), ())),
                                                      preferred_element_type=F32)
        n_ref[hd:hd + 1, :] = decay * n_prev + jnp.sum(kw, axis=0, keepdims=True)
        m_ref[hd:hd + 1, :] = jnp.broadcast_to(b_col[L - 1:L, :] + m_last, (1, LANES))


def _mlstm_call(mx, gcol, grow):
    rows = M_STEP_CHUNKS * M_CHUNK
    nc = SEQ // rows
    tok = lambda width: pl.BlockSpec((rows, width), lambda b, c: (b * nc + c, 0))
    return pl.pallas_call(
        _mlstm_kernel,
        grid=(BATCH, nc),
        in_specs=[tok(MX_WIDTH), tok(GC_WIDTH), pl.BlockSpec((SUBLANES, rows), lambda b, c: (0, b * nc + c))],
        out_specs=tok(M_WIDTH),
        out_shape=jax.ShapeDtypeStruct((N_TOK, M_WIDTH), BF16),
        scratch_shapes=[pltpu.VMEM((M_HEADS, M_HEAD_DIM, M_HEAD_DIM), F32),
                        pltpu.VMEM((SUBLANES, M_HEAD_DIM), F32),
                        pltpu.VMEM((SUBLANES, LANES), F32)],
        compiler_params=pltpu.CompilerParams(dimension_semantics=("arbitrary", "arbitrary"),
                                             vmem_limit_bytes=VMEM_LIMIT),
        name="mlstm",
    )(mx, gcol, grow)


def _swa_kernel(ax_ref, prev_ref, sink_ref, o_ref):
    n = pl.program_id(1)
    blk = A_BLOCK
    tri = lax.broadcasted_iota(jnp.int32, (blk, blk), 0) >= lax.broadcasted_iota(jnp.int32, (blk, blk), 1)
    first_bias = jnp.where(n > 0, 0.0, -jnp.inf)
    low_kv = lax.broadcasted_iota(jnp.int32, (2 * blk, LANES), 1) < A_HEAD_DIM
    low_out = lax.broadcasted_iota(jnp.int32, (blk, LANES), 1) < A_HEAD_DIM
    ones = jnp.ones((2 * blk, LANES), BF16)
    zero_kv = jnp.zeros((2 * blk, LANES), BF16)
    for sb in range(A_STEP_BLOCKS):
        rows = slice(sb * blk, (sb + 1) * blk)
        prows = slice((sb - 1) * blk, sb * blk)
        for hk in range(A_KV_HEADS):
            kcols = slice(AX_K + hk * LANES, AX_K + (hk + 1) * LANES)
            vcols = slice(AX_V + hk * LANES, AX_V + (hk + 1) * LANES)
            k_prev = prev_ref[:, slice(kcols.start - AX_K, kcols.stop - AX_K)] if sb == 0 else ax_ref[prows, kcols]
            v_prev = prev_ref[:, slice(vcols.start - AX_K, vcols.stop - AX_K)] if sb == 0 else ax_ref[prows, vcols]
            kd = jnp.concatenate([ax_ref[rows, kcols], k_prev], axis=0)
            vd = jnp.concatenate([ax_ref[rows, vcols], v_prev], axis=0)
            k_half = (jnp.where(low_kv, kd, zero_kv), jnp.where(low_kv, zero_kv, kd))
            v_ext = jnp.concatenate([vd, ones], axis=1)
            pairs = [hk * (A_GROUP // 2) + j for j in range(A_GROUP // 2)]
            qcat = jnp.concatenate([ax_ref[rows, pr * LANES:(pr + 1) * LANES] for pr in pairs], axis=0)
            probs, sink_w = {}, {}
            for half in range(2):
                s2 = lax.dot_general(qcat, k_half[half], (((1,), (1,)), ((), ())), preferred_element_type=F32)
                for j, pr in enumerate(pairs):
                    hq = 2 * pr + half
                    sj = s2[j * blk:(j + 1) * blk]
                    s_prev = sj[:, blk:] + first_bias if sb == 0 else sj[:, blk:]
                    s = jnp.where(tri, sj[:, :blk], s_prev)
                    sink = sink_ref[hq:hq + 1, 0:1]
                    mx = jnp.maximum(jnp.broadcast_to(jnp.max(s, axis=-1, keepdims=True), (blk, blk)), sink)
                    p = jnp.exp2(s - mx)
                    sink_w[hq] = jnp.exp2(sink - mx)
                    probs[hq] = jnp.concatenate([jnp.where(tri, p, 0.0), jnp.where(tri, 0.0, p)],
                                                axis=1).astype(BF16)
            heads = sorted(probs)
            o2 = jnp.dot(jnp.concatenate([probs[hq] for hq in heads], axis=0), v_ext,
                         preferred_element_type=F32)
            outs = {}
            for r, hq in enumerate(heads):
                oh = o2[r * blk:(r + 1) * blk]
                outs[hq] = oh[:, :LANES] / (oh[:, LANES:] + sink_w[hq])
            for pr in pairs:
                o_ref[rows, pr * LANES:(pr + 1) * LANES] = jnp.where(low_out, outs[2 * pr],
                                                                      outs[2 * pr + 1]).astype(BF16)


def _swa_call(ax, layer, sinks):
    nb = SEQ // A_BLOCK
    ns = nb // A_STEP_BLOCKS
    rows = A_STEP_BLOCKS * A_BLOCK
    cur = lambda width: pl.BlockSpec((rows, width), lambda b, n: (b * ns + n, 0))
    prev = pl.BlockSpec((A_BLOCK, AX_WIDTH - AX_K),
                        lambda b, n: (b * nb + jnp.maximum(n * A_STEP_BLOCKS - 1, 0), AX_K // (AX_WIDTH - AX_K)))
    return pl.pallas_call(
        _swa_kernel,
        grid=(BATCH, ns),
        in_specs=[cur(AX_WIDTH), prev, _layer_spec(layer, (A_Q_HEADS, LANES))],
        out_specs=cur(A_WIDTH),
        out_shape=jax.ShapeDtypeStruct((N_TOK, A_WIDTH), BF16),
        compiler_params=pltpu.CompilerParams(dimension_semantics=("arbitrary", "arbitrary"),
                                             vmem_limit_bytes=VMEM_LIMIT),
        name="swa",
    )(ax, ax, sinks)


def _merge_kernel(h_ref, hm_ref, ha_ref, g_ref, wgm_ref, wga_ref, wm_ref, wa_ref, wo_ref, o_ref):
    x = h_ref[...]
    u = (x * _rms_scale(x) * g_ref[...]).astype(BF16)
    gm = _sigmoid(jnp.dot(u, wgm_ref[...], preferred_element_type=F32))
    bm = jnp.dot(hm_ref[...], wm_ref[...], preferred_element_type=F32)
    ga = _sigmoid(jnp.dot(u, wga_ref[...], preferred_element_type=F32))
    ba = jnp.dot(ha_ref[...], wa_ref[...], preferred_element_type=F32)
    merged = (gm * bm + ga * ba).astype(BF16)
    o_ref[...] = x + jnp.dot(merged, wo_ref[...], preferred_element_type=F32)


def _merge_call(h, hm, ha, layer, w):
    tok = pl.BlockSpec((TM, D_MODEL), lambda i: (i, 0))
    wspec = _layer_spec(layer, (D_MODEL, D_MODEL))
    gate_spec = lambda name: _layer_spec(layer, (D_MODEL, D_MODEL), (0, W_IN_OFFS[name][0] // D_MODEL))
    return pl.pallas_call(
        _merge_kernel,
        grid=(N_TOK // TM,),
        in_specs=[tok, tok, tok, _layer_spec(layer, (1, D_MODEL)), gate_spec("gm"), gate_spec("ga"),
                  wspec, wspec, wspec],
        out_specs=tok,
        out_shape=jax.ShapeDtypeStruct((N_TOK, D_MODEL), F32),
        compiler_params=pltpu.CompilerParams(dimension_semantics=("arbitrary",),
                                             vmem_limit_bytes=VMEM_LIMIT),
        name="merge",
    )(h, hm, ha, w["mix_norm"], w["w_in"], w["w_in"], w["wm"], w["wa"], w["wout"])


W_IN_OFFS = {"qk": (0, 2 * M_WIDTH), "v": (2048, M_WIDTH), "o": (3072, M_WIDTH), "gm": (4096, D_MODEL),
             "ga": (5120, D_MODEL), "aq": (6144, A_WIDTH), "ak": (7168, A_KV_WIDTH), "av": (7424, A_KV_WIDTH)}


def _head_sum_matrix():
    idx = np.arange(MXU_DIM) // A_HEAD_DIM
    return jnp.asarray(idx[:, None] == idx[None, :], dtype=BF16)


def _prep(ffn1_norm, ffn1_w_gate, ffn1_w_up, ffn1_w_down, mix_norm, w_in, m_conv_w, m_conv_b,
          m_igate_b, m_fgate_b, m_out_norm, a_q_norm, a_k_norm, a_sinks, w_branch_m, w_branch_a,
          w_out, ffn2_norm, ffn2_w_gate, ffn2_w_up, ffn2_w_down, ple_norm, ple_gate_w, ple_proj_w):
    ff_pad = D_FF_PAD - D_FF
    row = lambda v: v.reshape(DEPTH, 1, -1).astype(F32)
    wide = lambda wt: jnp.pad(wt, ((0, 0), (0, 0), (0, ff_pad))).astype(BF16)
    tall = lambda wt: jnp.pad(wt, ((0, 0), (0, ff_pad), (0, 0))).astype(BF16)
    sizes = [2 * M_WIDTH, M_WIDTH, M_WIDTH, M_HEADS, M_HEADS, A_WIDTH, A_KV_WIDTH, A_KV_WIDTH, D_MODEL, D_MODEL]
    offs = np.concatenate([[0], np.cumsum(sizes)])
    part = lambda j: w_in[:, :, int(offs[j]):int(offs[j + 1])]
    w_all = jnp.concatenate([part(0), part(1), part(2), part(8), part(9), part(5), part(6), part(7)],
                            axis=-1).astype(BF16)
    lane_pad = lambda wt: jnp.pad(wt, ((0, 0), (0, 0), (0, LANES - M_HEADS)))
    sub_pad = lambda wt: jnp.pad(wt, ((0, 0), (0, SUBLANES - M_HEADS), (0, 0)))
    w_i, w_f = part(3).astype(BF16), part(4).astype(BF16)
    b_i, b_f = m_igate_b.astype(F32), m_fgate_b.astype(F32)
    return dict(
        ffn1=(row(ffn1_norm), wide(ffn1_w_gate), wide(ffn1_w_up), tall(ffn1_w_down)),
        ffn2=(row(ffn2_norm), wide(ffn2_w_gate), wide(ffn2_w_up), tall(ffn2_w_down)),
        ple=(row(ple_norm), ple_gate_w.astype(BF16), ple_proj_w.astype(BF16)),
        mix_norm=row(mix_norm), w_in=w_all,
        wif=jnp.concatenate([lane_pad(w_i), lane_pad(w_f)], axis=-1),
        wift=jnp.concatenate([sub_pad(jnp.swapaxes(w_i, 1, 2)), sub_pad(jnp.swapaxes(w_f, 1, 2))], axis=1),
        gate_b_col=jnp.concatenate([lane_pad(b_i[:, None, :]), lane_pad(b_f[:, None, :])], axis=-1),
        gate_b_row=jnp.concatenate([sub_pad(b_i[:, :, None]), sub_pad(b_f[:, :, None])], axis=1),
        conv_w=m_conv_w.astype(F32), conv_b=row(m_conv_b), head_sum=_head_sum_matrix(),
        q_gain=row(jnp.tile(a_q_norm, (1, A_Q_HEADS)) * (A_HEAD_DIM ** -0.5 * LOG2E)),
        k_gain=row(jnp.tile(a_k_norm, (1, A_KV_HEADS))),
        out_gain=row(m_out_norm),
        sinks=jnp.broadcast_to((a_sinks.astype(F32) * LOG2E)[:, :, None], (DEPTH, A_Q_HEADS, LANES)),
        wm=w_branch_m.astype(BF16), wa=w_branch_a.astype(BF16), wout=w_out.astype(BF16),
    )


def kernel(x, p, ffn1_norm, ffn1_w_gate, ffn1_w_up, ffn1_w_down, mix_norm, w_in, m_conv_w, m_conv_b, m_igate_b, m_fgate_b, m_out_norm, a_q_norm, a_k_norm, a_sinks, w_branch_m, w_branch_a, w_out, ffn2_norm, ffn2_w_gate, ffn2_w_up, ffn2_w_down, ple_norm, ple_gate_w, ple_proj_w):
    w = _prep(ffn1_norm, ffn1_w_gate, ffn1_w_up, ffn1_w_down, mix_norm, w_in, m_conv_w, m_conv_b, m_igate_b,
              m_fgate_b, m_out_norm, a_q_norm, a_k_norm, a_sinks, w_branch_m, w_branch_a, w_out, ffn2_norm,
              ffn2_w_gate, ffn2_w_up, ffn2_w_down, ple_norm, ple_gate_w, ple_proj_w)
    h = x.reshape(N_TOK, D_MODEL)
    p_tok = p.reshape(DEPTH, N_TOK, PLE_DIM)
    for i in range(DEPTH):
        h = _ffn_call(h, i, *w["ffn1"])
        mx, gcol, grow, ax = _inproj_call(h, i, w)
        hm = _mlstm_call(mx, gcol, grow)
        ha = _swa_call(ax, i, w["sinks"])
        h = _merge_call(h, hm, ha, i, w)
        pg, wpg, wpp = w["ple"]
        h = _ffn_call(h, i, *w["ffn2"], ple=(p_tok, pg, wpg, wpp))
    return h.reshape(BATCH, SEQ, D_MODEL)
```

```python
import math

import jax
import jax.numpy as jnp
import numpy as np
from jax import lax
from jax.experimental import pallas as pl
from jax.experimental.pallas import tpu as pltpu

D_MODEL = 1024
BATCH = 8
SEQ = 4096
DEPTH = 2
PLE_DIM = 256
D_FF = 2752
M_HEADS = 4
M_HEAD_DIM = 256
M_WIDTH = M_HEADS * M_HEAD_DIM
CONV_K = 4
A_Q_HEADS = 16
A_KV_HEADS = 4
A_HEAD_DIM = 64
A_GROUP = A_Q_HEADS // A_KV_HEADS
A_WIDTH = A_Q_HEADS * A_HEAD_DIM
A_KV_WIDTH = A_KV_HEADS * A_HEAD_DIM
WINDOW = 128
A_BLOCK = 128
EPS = 1e-6

N_TOK = BATCH * SEQ
LANES = 128
SUBLANES = 8
MXU_DIM = 256
D_FF_PAD = -(-D_FF // MXU_DIM) * MXU_DIM
TM = 512
M_CHUNK = 256
M_ROWS = 128
M_STEP_CHUNKS = 4
A_STEP_BLOCKS = 8
A_KV_DUP = A_KV_HEADS * LANES
VMEM_LIMIT = 56 * 1024 * 1024
MX_Q, MX_K, MX_V, MX_O, MX_WIDTH = 0, M_WIDTH, 2 * M_WIDTH, 3 * M_WIDTH, 4 * M_WIDTH
GC_A, GC_B, GC_M, GC_WIDTH = 0, LANES, 2 * LANES, 3 * LANES
AX_Q, AX_K, AX_V, AX_WIDTH = 0, A_WIDTH, A_WIDTH + A_KV_DUP, A_WIDTH + 2 * A_KV_DUP
SG_M, SG_A, SG_WIDTH = 0, D_MODEL, 2 * D_MODEL
LOG2E = math.log2(math.e)

BF16 = jnp.bfloat16
F32 = jnp.float32


def _layer_spec(layer, block, index=None):
    index = (layer,) + tuple(index if index is not None else (0,) * len(block))
    return pl.BlockSpec((None,) + tuple(block), lambda *_: index, pipeline_mode=pl.Buffered(1))


def _rms_scale(x):
    return lax.rsqrt(jnp.mean(x * x, axis=-1, keepdims=True) + EPS)


def _log_sigmoid(x):
    return jnp.minimum(x, 0.0) - jnp.log1p(jnp.exp(-jnp.abs(x)))


def _sigmoid(x):
    return 0.5 * jnp.tanh(0.5 * x) + 0.5


def _dup_heads_lanes(x):
    low = lax.broadcasted_iota(jnp.int32, (x.shape[0], LANES), 1) < A_HEAD_DIM
    out = []
    for j in range(x.shape[1] // LANES):
        pair = x[:, j * LANES:(j + 1) * LANES]
        swapped = pltpu.roll(pair, A_HEAD_DIM, 1)
        out += [jnp.where(low, pair, swapped), jnp.where(low, swapped, pair)]
    return jnp.concatenate(out, axis=1)


def _segment_scan(x, axis, seg, op, fill):
    pos = lax.broadcasted_iota(jnp.int32, x.shape, axis) % seg
    shift = 1
    while shift < seg:
        x = op(x, jnp.where(pos >= shift, pltpu.roll(x, shift, axis), fill))
        shift *= 2
    return x


def _ffn_body(x, g_ref, wg_ref, wu_ref, wd_ref):
    xn = (x * _rms_scale(x) * g_ref[...]).astype(BF16)
    y = jnp.zeros(x.shape, F32)
    for c in range(D_FF_PAD // MXU_DIM):
        cols = slice(c * MXU_DIM, (c + 1) * MXU_DIM)
        gate = jnp.dot(xn, wg_ref[:, cols], preferred_element_type=F32)
        up = jnp.dot(xn, wu_ref[:, cols], preferred_element_type=F32)
        act = (gate * jax.nn.sigmoid(gate) * up).astype(BF16)
        y = y + jnp.dot(act, wd_ref[cols, :], preferred_element_type=F32)
    return x + 0.5 * y


def _ffn_kernel(x_ref, g_ref, wg_ref, wu_ref, wd_ref, o_ref):
    o_ref[...] = _ffn_body(x_ref[...], g_ref, wg_ref, wu_ref, wd_ref)


def _ffn_ple_kernel(x_ref, g_ref, wg_ref, wu_ref, wd_ref, p_ref, pg_ref, wpg_ref, wpp_ref, o_ref):
    h = _ffn_body(x_ref[...], g_ref, wg_ref, wu_ref, wd_ref)
    hn = (h * _rms_scale(h) * pg_ref[...]).astype(BF16)
    gate = jax.nn.sigmoid(jnp.dot(hn, wpg_ref[...], preferred_element_type=F32))
    emb = jnp.dot(p_ref[...].astype(BF16), wpp_ref[...], preferred_element_type=F32)
    o_ref[...] = h + gate * emb


def _ffn_call(h, layer, g, wg, wu, wd, ple=None):
    tok = lambda w: pl.BlockSpec((TM, w), lambda i: (i, 0))
    in_specs = [tok(D_MODEL), _layer_spec(layer, (1, D_MODEL)), _layer_spec(layer, (D_MODEL, D_FF_PAD)),
                _layer_spec(layer, (D_MODEL, D_FF_PAD)), _layer_spec(layer, (D_FF_PAD, D_MODEL))]
    args = [h, g, wg, wu, wd]
    body = _ffn_kernel
    if ple is not None:
        p, pg, wpg, wpp = ple
        in_specs += [pl.BlockSpec((None, TM, PLE_DIM), lambda i: (layer, i, 0)), _layer_spec(layer, (1, D_MODEL)),
                     _layer_spec(layer, (D_MODEL, D_MODEL)), _layer_spec(layer, (PLE_DIM, D_MODEL))]
        args += [p, pg, wpg, wpp]
        body = _ffn_ple_kernel
    return pl.pallas_call(
        body,
        grid=(N_TOK // TM,),
        in_specs=in_specs,
        out_specs=tok(D_MODEL),
        out_shape=jax.ShapeDtypeStruct((N_TOK, D_MODEL), F32),
        compiler_params=pltpu.CompilerParams(dimension_semantics=("arbitrary",),
                                             vmem_limit_bytes=VMEM_LIMIT),
        name="ffn_ple" if ple is not None else "ffn",
    )(*args)


CONV_COLS = 256
CONV_ROWS = 64
ROW_SPLIT = 2
PLAIN_PARTS = (2 * M_WIDTH // CONV_COLS) // 4
PLAIN_COLS = D_MODEL // PLAIN_PARTS


def _inproj_kernel(h_ref, g_ref, wqk_ref, wv_ref, wo_ref, wif_ref, wift_ref, waq_ref, wak_ref, wav_ref,
                   wgm_ref, wga_ref, cw_ref, cb_ref, gbc_ref, gbr_ref, pq_ref, qg_ref, kg_ref, og_ref,
                   mx_ref, gcol_ref, grow_ref, ax_ref, sg_ref, *scratch):
    zbufs, stages = scratch[:-2], scratch[-2:]
    i = pl.program_id(0)

    @pl.when(i % (SEQ // TM) == 0)
    def _():
        for zb in zbufs:
            zb[0:SUBLANES, :] = jnp.zeros((SUBLANES, CONV_COLS), F32)

    x = h_ref[...]
    u = (x * _rms_scale(x) * g_ref[...]).astype(BF16)

    n_groups = 2 * M_WIDTH // CONV_COLS

    half = TM // ROW_SPLIT
    periods = [(c, hf) for c in range(n_groups) for hf in range(ROW_SPLIT)]

    def issue_dots(idx):
        c, hf = periods[idx]
        uh = u[hf * half:(hf + 1) * half]
        cols = slice(c * CONV_COLS, (c + 1) * CONV_COLS)
        zbufs[c][SUBLANES + hf * half:SUBLANES + (hf + 1) * half, :] = jnp.dot(
            uh, wqk_ref[:, cols], preferred_element_type=F32)
        which, part = divmod(c, PLAIN_PARTS)
        pc = slice(part * PLAIN_COLS, (part + 1) * PLAIN_COLS)
        stages[idx % 2][...] = jnp.dot(uh, (wv_ref, wo_ref, wgm_ref, wga_ref)[which][:, pc],
                                       preferred_element_type=F32)

    issue_dots(0)
    for idx, (c, hf) in enumerate(periods):
        if idx + 1 < len(periods):
            issue_dots(idx + 1)
        cols = slice(c * CONV_COLS, (c + 1) * CONV_COLS)
        which, part = divmod(c, PLAIN_PARTS)
        pc = slice(part * PLAIN_COLS, (part + 1) * PLAIN_COLS)
        st, zb = stages[idx % 2], zbufs[c]
        for rb in range(half // CONV_ROWS):
            row0 = hf * half + rb * CONV_ROWS
            win = zb[row0:row0 + SUBLANES + CONV_ROWS, :]
            win1 = pltpu.roll(win, 1, 0)
            near = win * cw_ref[3:4, cols] + win1 * cw_ref[2:3, cols]
            far = win * cw_ref[1:2, cols] + win1 * cw_ref[0:1, cols]
            acc = (near + pltpu.roll(far, 2, 0))[SUBLANES:] + cb_ref[:, cols]
            half_acc = 0.5 * acc
            y = half_acc * jnp.tanh(half_acc) + half_acc
            rows = slice(row0, row0 + CONV_ROWS)
            if c < M_WIDTH // CONV_COLS:
                mx_ref[rows, cols] = (y * (M_HEAD_DIM ** -0.5)).astype(BF16)
            else:
                mx_ref[rows, cols] = y.astype(BF16)
        if hf == ROW_SPLIT - 1:
            zb[0:SUBLANES, :] = zb[TM:TM + SUBLANES, :]
        for rb in range(half // CONV_ROWS):
            rows = slice(hf * half + rb * CONV_ROWS, hf * half + (rb + 1) * CONV_ROWS)
            r = st[rb * CONV_ROWS:(rb + 1) * CONV_ROWS, :]
            if which == 0:
                mx_ref[rows, slice(MX_V + pc.start, MX_V + pc.stop)] = r.astype(BF16)
            elif which == 1:
                gated = _sigmoid(r) * og_ref[:, pc]
                mx_ref[rows, slice(MX_O + pc.start, MX_O + pc.stop)] = gated.astype(BF16)
            elif which == 2:
                sg_ref[rows, slice(SG_M + pc.start, SG_M + pc.stop)] = _sigmoid(r).astype(BF16)
            else:
                sg_ref[rows, slice(SG_A + pc.start, SG_A + pc.stop)] = _sigmoid(r).astype(BF16)

    zc = jnp.dot(u, wif_ref[...], preferred_element_type=F32) + gbc_ref[...]
    b_col = _segment_scan(_log_sigmoid(zc[:, LANES:]) * LOG2E, 0, M_CHUNK, jnp.add, 0.0)
    a_col = zc[:, :LANES] * LOG2E - b_col
    gcol_ref[:, GC_A:GC_B] = a_col
    gcol_ref[:, GC_B:GC_M] = b_col
    gcol_ref[:, GC_M:GC_WIDTH] = _segment_scan(a_col, 0, M_CHUNK, jnp.maximum, -jnp.inf)
    zr = lax.dot_general(wift_ref[...], u, (((1,), (1,)), ((), ())), preferred_element_type=F32) + gbr_ref[...]
    b_row = _segment_scan(_log_sigmoid(zr[SUBLANES:]) * LOG2E, 1, M_CHUNK, jnp.add, 0.0)
    grow_ref[...] = zr[:SUBLANES] * LOG2E - b_row

    head_sum = pq_ref[...]

    def head_norm(z):
        ss = jnp.dot((z * z).astype(BF16), head_sum, preferred_element_type=F32)
        return z * lax.rsqrt(ss * (1.0 / A_HEAD_DIM) + EPS)

    zq = jnp.dot(u, waq_ref[...], preferred_element_type=F32)
    for g in range(A_WIDTH // MXU_DIM):
        gc = slice(g * MXU_DIM, (g + 1) * MXU_DIM)
        ax_ref[:, gc] = (head_norm(zq[:, gc]) * qg_ref[:, gc]).astype(BF16)
    zk = jnp.dot(u, wak_ref[...], preferred_element_type=F32)
    ax_ref[:, AX_K:AX_V] = _dup_heads_lanes(head_norm(zk) * kg_ref[...]).astype(BF16)
    zv = jnp.dot(u, wav_ref[...], preferred_element_type=F32)
    ax_ref[:, AX_V:AX_WIDTH] = _dup_heads_lanes(zv).astype(BF16)


def _inproj_call(h, layer, w):
    tok = lambda width: pl.BlockSpec((TM, width), lambda i: (i, 0))
    w_all = w["w_in"]

    def group(name):
        off, width = W_IN_OFFS[name]
        return w_all, _layer_spec(layer, (D_MODEL, width), (0, off // width))

    whole = lambda name: (w[name], _layer_spec(layer, w[name].shape[1:]))
    consts = [whole("mix_norm"), group("qk"), group("v"), group("o"), whole("wif"), whole("wift"), group("aq"),
              group("ak"), group("av"), group("gm"), group("ga"), whole("conv_w"), whole("conv_b"),
              whole("gate_b_col"), whole("gate_b_row"),
              (w["head_sum"], pl.BlockSpec((MXU_DIM, MXU_DIM), lambda i: (0, 0), pipeline_mode=pl.Buffered(1))),
              whole("q_gain"), whole("k_gain"), whole("out_gain")]
    out_shape = [
        jax.ShapeDtypeStruct((N_TOK, MX_WIDTH), BF16),
        jax.ShapeDtypeStruct((N_TOK, GC_WIDTH), F32),
        jax.ShapeDtypeStruct((SUBLANES, N_TOK), F32),
        jax.ShapeDtypeStruct((N_TOK, AX_WIDTH), BF16),
        jax.ShapeDtypeStruct((N_TOK, SG_WIDTH), BF16),
    ]
    out_specs = [tok(MX_WIDTH), tok(GC_WIDTH), pl.BlockSpec((SUBLANES, TM), lambda i: (0, i)),
                 tok(AX_WIDTH), tok(SG_WIDTH)]
    return pl.pallas_call(
        _inproj_kernel,
        grid=(N_TOK // TM,),
        in_specs=[tok(D_MODEL)] + [spec for _, spec in consts],
        out_specs=out_specs,
        out_shape=out_shape,
        scratch_shapes=([pltpu.VMEM((SUBLANES + TM, CONV_COLS), F32)] * (2 * M_WIDTH // CONV_COLS)
                        + [pltpu.VMEM((TM // ROW_SPLIT, PLAIN_COLS), F32)] * 2),
        compiler_params=pltpu.CompilerParams(dimension_semantics=("arbitrary",),
                                             vmem_limit_bytes=VMEM_LIMIT),
        name="inproj",
    )(h, *[arr for arr, _ in consts])


def _mlstm_kernel(mx_ref, gcol_ref, grow_ref, o_ref, c_ref, n_ref, m_ref):
    L, R = M_CHUNK, M_ROWS

    @pl.when(pl.program_id(1) == 0)
    def _():
        c_ref[...] = jnp.zeros(c_ref.shape, F32)
        n_ref[...] = jnp.zeros(n_ref.shape, F32)
        m_ref[...] = jnp.zeros(m_ref.shape, F32)

    tri = lax.broadcasted_iota(jnp.int32, (R, R), 0) >= lax.broadcasted_iota(jnp.int32, (R, R), 1)
    lane = lax.broadcasted_iota(jnp.int32, (L, LANES), 1)
    for ch in range(M_STEP_CHUNKS):
        _mlstm_chunk(ch * L, tri, lane, mx_ref, gcol_ref, grow_ref, o_ref, c_ref, n_ref, m_ref)


def _mlstm_chunk(base, tri, lane, mx_ref, gcol_ref, grow_ref, o_ref, c_ref, n_ref, m_ref):
    L, R = M_CHUNK, M_ROWS
    chunk = slice(base, base + L)
    ga, gb, gm = gcol_ref[chunk, GC_A:GC_B], gcol_ref[chunk, GC_B:GC_M], gcol_ref[chunk, GC_M:GC_WIDTH]
    for hd in range(M_HEADS):
        cols = slice(hd * M_HEAD_DIM, (hd + 1) * M_HEAD_DIM)
        qcols, kcols, vcols, ocols = (slice(off + cols.start, off + cols.stop) for off in (MX_Q, MX_K, MX_V, MX_O))
        a_row = grow_ref[hd:hd + 1, chunk]
        pick = lane == hd
        a_col = jnp.sum(jnp.where(pick, ga, 0.0), axis=-1, keepdims=True)
        b_col = jnp.sum(jnp.where(pick, gb, 0.0), axis=-1, keepdims=True)
        cm_col = jnp.sum(jnp.where(pick, gm, 0.0), axis=-1, keepdims=True)
        m_prev = jnp.max(m_ref[hd:hd + 1, :], axis=-1, keepdims=True)
        n_prev = n_ref[hd:hd + 1, :]
        ck_prev = c_ref[hd]
        ck_bf = ck_prev.astype(BF16)
        m_loc = jnp.maximum(cm_col, m_prev)
        w_inter = jnp.exp2(m_prev - m_loc)
        floor = jnp.exp2(-(b_col + m_loc))
        k = mx_ref[chunk, kcols]
        v = mx_ref[chunk, vcols]
        for rb in range(L // R):
            rows = slice(rb * R, (rb + 1) * R)
            orows = slice(base + rb * R, base + (rb + 1) * R)
            width = (rb + 1) * R
            q = mx_ref[orows, qcols]
            s = lax.dot_general(q, k[:width], (((1,), (1,)), ((), ())), preferred_element_type=F32)
            ml = m_loc[rows]
            wi = w_inter[rows]
            blocks = []
            for cb in range(rb + 1):
                e = jnp.exp2(a_row[:, cb * R:(cb + 1) * R] - ml)
                if cb == rb:
                    e = jnp.where(tri, e, 0.0)
                blocks.append(s[:, cb * R:(cb + 1) * R] * e)
            sw = blocks[0] if rb == 0 else jnp.concatenate(blocks, axis=1)
            num = (jnp.dot(sw.astype(BF16), v[:width], preferred_element_type=F32)
                   + wi * jnp.dot(q, ck_bf, preferred_element_type=F32))
            qn = q.astype(F32) * n_prev
            dsum = wi * (qn[:, :LANES] + qn[:, LANES:])
            for blk in blocks:
                dsum = dsum + blk
            den = jnp.sum(dsum, axis=-1, keepdims=True)
            dd = jnp.maximum(jnp.abs(den), floor[rows])
            scale = lax.rsqrt(jnp.mean(num * num, axis=-1, keepdims=True) + EPS * (dd * dd))
            o_ref[orows, cols] = (num * scale * mx_ref[orows, ocols]).astype(BF16)

        m_last = m_loc[L - 1:L, :]
        kw = k.astype(F32) * jnp.exp2(a_col - m_last)
        decay = jnp.exp2(m_prev - m_last)
        c_ref[hd] = decay * ck_prev + lax.dot_general(kw.astype(BF16), v, (((0,), (0,)), ((), ())),
                                                      preferred_element_type=F32)
        n_ref[hd:hd + 1, :] = decay * n_prev + jnp.sum(kw, axis=0, keepdims=True)
        m_ref[hd:hd + 1, :] = jnp.broadcast_to(b_col[L - 1:L, :] + m_last, (1, LANES))


def _mlstm_call(mx, gcol, grow):
    rows = M_STEP_CHUNKS * M_CHUNK
    nc = SEQ // rows
    tok = lambda width: pl.BlockSpec((rows, width), lambda b, c: (b * nc + c, 0))
    return pl.pallas_call(
        _mlstm_kernel,
        grid=(BATCH, nc),
        in_specs=[tok(MX_WIDTH), tok(GC_WIDTH), pl.BlockSpec((SUBLANES, rows), lambda b, c: (0, b * nc + c))],
        out_specs=tok(M_WIDTH),
        out_shape=jax.ShapeDtypeStruct((N_TOK, M_WIDTH), BF16),
        scratch_shapes=[pltpu.VMEM((M_HEADS, M_HEAD_DIM, M_HEAD_DIM), F32),
                        pltpu.VMEM((SUBLANES, M_HEAD_DIM), F32),
                        pltpu.VMEM((SUBLANES, LANES), F32)],
        compiler_params=pltpu.CompilerParams(dimension_semantics=("arbitrary", "arbitrary"),
                                             vmem_limit_bytes=VMEM_LIMIT),
        name="mlstm",
    )(mx, gcol, grow)


def _swa_kernel(ax_ref, prev_ref, sink_ref, o_ref):
    n = pl.program_id(1)
    blk = A_BLOCK
    tri = lax.broadcasted_iota(jnp.int32, (blk, blk), 0) >= lax.broadcasted_iota(jnp.int32, (blk, blk), 1)
    first_bias = jnp.where(n > 0, 0.0, -jnp.inf)
    low_kv = lax.broadcasted_iota(jnp.int32, (2 * blk, LANES), 1) < A_HEAD_DIM
    low_out = lax.broadcasted_iota(jnp.int32, (blk, LANES), 1) < A_HEAD_DIM
    ones = jnp.ones((2 * blk, LANES), BF16)
    zero_kv = jnp.zeros((2 * blk, LANES), BF16)
    for sb in range(A_STEP_BLOCKS):
        rows = slice(sb * blk, (sb + 1) * blk)
        prows = slice((sb - 1) * blk, sb * blk)
        for hk in range(A_KV_HEADS):
            kcols = slice(AX_K + hk * LANES, AX_K + (hk + 1) * LANES)
            vcols = slice(AX_V + hk * LANES, AX_V + (hk + 1) * LANES)
            k_prev = prev_ref[:, slice(kcols.start - AX_K, kcols.stop - AX_K)] if sb == 0 else ax_ref[prows, kcols]
            v_prev = prev_ref[:, slice(vcols.start - AX_K, vcols.stop - AX_K)] if sb == 0 else ax_ref[prows, vcols]
            kd = jnp.concatenate([ax_ref[rows, kcols], k_prev], axis=0)
            vd = jnp.concatenate([ax_ref[rows, vcols], v_prev], axis=0)
            k_half = (jnp.where(low_kv, kd, zero_kv), jnp.where(low_kv, zero_kv, kd))
            v_ext = jnp.concatenate([vd, ones], axis=1)
            pairs = [hk * (A_GROUP // 2) + j for j in range(A_GROUP // 2)]
            qcat = jnp.concatenate([ax_ref[rows, pr * LANES:(pr + 1) * LANES] for pr in pairs], axis=0)
            probs, sink_w = {}, {}
            for half in range(2):
                s2 = lax.dot_general(qcat, k_half[half], (((1,), (1,)), ((), ())), preferred_element_type=F32)
                for j, pr in enumerate(pairs):
                    hq = 2 * pr + half
                    sj = s2[j * blk:(j + 1) * blk]
                    s_prev = sj[:, blk:] + first_bias if sb == 0 else sj[:, blk:]
                    s = jnp.where(tri, sj[:, :blk], s_prev)
                    sink = sink_ref[hq:hq + 1, 0:1]
                    mx = jnp.maximum(jnp.broadcast_to(jnp.max(s, axis=-1, keepdims=True), (blk, blk)), sink)
                    p = jnp.exp2(s - mx)
                    sink_w[hq] = jnp.exp2(sink - mx)
                    probs[hq] = jnp.concatenate([jnp.where(tri, p, 0.0), jnp.where(tri, 0.0, p)],
                                                axis=1).astype(BF16)
            heads = sorted(probs)
            o2 = jnp.dot(jnp.concatenate([probs[hq] for hq in heads], axis=0), v_ext,
                         preferred_element_type=F32)
            outs = {}
            for r, hq in enumerate(heads):
                oh = o2[r * blk:(r + 1) * blk]
                outs[hq] = oh[:, :LANES] / (oh[:, LANES:] + sink_w[hq])
            for pr in pairs:
                o_ref[rows, pr * LANES:(pr + 1) * LANES] = jnp.where(low_out, outs[2 * pr],
                                                                      outs[2 * pr + 1]).astype(BF16)


def _swa_call(ax, layer, sinks):
    nb = SEQ // A_BLOCK
    ns = nb // A_STEP_BLOCKS
    rows = A_STEP_BLOCKS * A_BLOCK
    cur = lambda width: pl.BlockSpec((rows, width), lambda b, n: (b * ns + n, 0))
    prev = pl.BlockSpec((A_BLOCK, AX_WIDTH - AX_K),
                        lambda b, n: (b * nb + jnp.maximum(n * A_STEP_BLOCKS - 1, 0), AX_K // (AX_WIDTH - AX_K)))
    return pl.pallas_call(
        _swa_kernel,
        grid=(BATCH, ns),
        in_specs=[cur(AX_WIDTH), prev, _layer_spec(layer, (A_Q_HEADS, LANES))],
        out_specs=cur(A_WIDTH),
        out_shape=jax.ShapeDtypeStruct((N_TOK, A_WIDTH), BF16),
        compiler_params=pltpu.CompilerParams(dimension_semantics=("arbitrary", "arbitrary"),
                                             vmem_limit_bytes=VMEM_LIMIT),
        name="swa",
    )(ax, ax, sinks)


def _merge_kernel(h_ref, hm_ref, ha_ref, sg_ref, wm_ref, wa_ref, wo_ref, o_ref):
    bm = jnp.dot(hm_ref[...], wm_ref[...], preferred_element_type=F32)
    ba = jnp.dot(ha_ref[...], wa_ref[...], preferred_element_type=F32)
    merged = (sg_ref[:, SG_M:SG_A] * bm + sg_ref[:, SG_A:SG_WIDTH] * ba).astype(BF16)
    o_ref[...] = h_ref[...] + jnp.dot(merged, wo_ref[...], preferred_element_type=F32)


def _merge_call(h, hm, ha, sg, layer, wm, wa, wo):
    tok = pl.BlockSpec((TM, D_MODEL), lambda i: (i, 0))
    wspec = _layer_spec(layer, (D_MODEL, D_MODEL))
    return pl.pallas_call(
        _merge_kernel,
        grid=(N_TOK // TM,),
        in_specs=[tok, tok, tok, pl.BlockSpec((TM, SG_WIDTH), lambda i: (i, 0)), wspec, wspec, wspec],
        out_specs=tok,
        out_shape=jax.ShapeDtypeStruct((N_TOK, D_MODEL), F32),
        compiler_params=pltpu.CompilerParams(dimension_semantics=("arbitrary",),
                                             vmem_limit_bytes=VMEM_LIMIT),
        name="merge",
    )(h, hm, ha, sg, wm, wa, wo)


def _merge_ffn_ple_kernel(h_ref, hm_ref, ha_ref, sg_ref, wm_ref, wa_ref, wo_ref, g_ref, wg_ref, wu_ref, wd_ref,
                          p_ref, pg_ref, wpg_ref, wpp_ref, o_ref):
    bm = jnp.dot(hm_ref[...], wm_ref[...], preferred_element_type=F32)
    ba = jnp.dot(ha_ref[...], wa_ref[...], preferred_element_type=F32)
    merged = (sg_ref[:, SG_M:SG_A] * bm + sg_ref[:, SG_A:SG_WIDTH] * ba).astype(BF16)
    h1 = h_ref[...] + jnp.dot(merged, wo_ref[...], preferred_element_type=F32)
    h2 = _ffn_body(h1, g_ref, wg_ref, wu_ref, wd_ref)
    hn = (h2 * _rms_scale(h2) * pg_ref[...]).astype(BF16)
    gate = jax.nn.sigmoid(jnp.dot(hn, wpg_ref[...], preferred_element_type=F32))
    emb = jnp.dot(p_ref[...].astype(BF16), wpp_ref[...], preferred_element_type=F32)
    o_ref[...] = h2 + gate * emb


def _merge_ffn_ple_call(h, hm, ha, sg, layer, wm, wa, wo, g, wg, wu, wd, p, pg, wpg, wpp):
    tok = pl.BlockSpec((TM, D_MODEL), lambda i: (i, 0))
    wspec = _layer_spec(layer, (D_MODEL, D_MODEL))
    in_specs = [tok, tok, tok, pl.BlockSpec((TM, SG_WIDTH), lambda i: (i, 0)), wspec, wspec, wspec,
                _layer_spec(layer, (1, D_MODEL)), _layer_spec(layer, (D_MODEL, D_FF_PAD)),
                _layer_spec(layer, (D_MODEL, D_FF_PAD)), _layer_spec(layer, (D_FF_PAD, D_MODEL)),
                pl.BlockSpec((None, TM, PLE_DIM), lambda i: (layer, i, 0)), _layer_spec(layer, (1, D_MODEL)),
                wspec, _layer_spec(layer, (PLE_DIM, D_MODEL))]
    return pl.pallas_call(
        _merge_ffn_ple_kernel,
        grid=(N_TOK // TM,),
        in_specs=in_specs,
        out_specs=tok,
        out_shape=jax.ShapeDtypeStruct((N_TOK, D_MODEL), F32),
        compiler_params=pltpu.CompilerParams(dimension_semantics=("arbitrary",),
                                             vmem_limit_bytes=VMEM_LIMIT),
        name="merge_ffn_ple",
    )(h, hm, ha, sg, wm, wa, wo, g, wg, wu, wd, p, pg, wpg, wpp)


W_IN_OFFS = {"qk": (0, 2 * M_WIDTH), "v": (2048, M_WIDTH), "o": (3072, M_WIDTH), "gm": (4096, D_MODEL),
             "ga": (5120, D_MODEL), "aq": (6144, A_WIDTH), "ak": (7168, A_KV_WIDTH), "av": (7424, A_KV_WIDTH)}


def _head_sum_matrix():
    idx = np.arange(MXU_DIM) // A_HEAD_DIM
    return jnp.asarray(idx[:, None] == idx[None, :], dtype=BF16)


def _prep(ffn1_norm, ffn1_w_gate, ffn1_w_up, ffn1_w_down, mix_norm, w_in, m_conv_w, m_conv_b,
          m_igate_b, m_fgate_b, m_out_norm, a_q_norm, a_k_norm, a_sinks, w_branch_m, w_branch_a,
          w_out, ffn2_norm, ffn2_w_gate, ffn2_w_up, ffn2_w_down, ple_norm, ple_gate_w, ple_proj_w):
    ff_pad = D_FF_PAD - D_FF
    row = lambda v: v.reshape(DEPTH, 1, -1).astype(F32)
    wide = lambda wt: jnp.pad(wt, ((0, 0), (0, 0), (0, ff_pad))).astype(BF16)
    tall = lambda wt: jnp.pad(wt, ((0, 0), (0, ff_pad), (0, 0))).astype(BF16)
    sizes = [2 * M_WIDTH, M_WIDTH, M_WIDTH, M_HEADS, M_HEADS, A_WIDTH, A_KV_WIDTH, A_KV_WIDTH, D_MODEL, D_MODEL]
    offs = np.concatenate([[0], np.cumsum(sizes)])
    part = lambda j: w_in[:, :, int(offs[j]):int(offs[j + 1])]
    w_all = jnp.concatenate([part(0), part(1), part(2), part(8), part(9), part(5), part(6), part(7)],
                            axis=-1).astype(BF16)
    lane_pad = lambda wt: jnp.pad(wt, ((0, 0), (0, 0), (0, LANES - M_HEADS)))
    sub_pad = lambda wt: jnp.pad(wt, ((0, 0), (0, SUBLANES - M_HEADS), (0, 0)))
    w_i, w_f = part(3).astype(BF16), part(4).astype(BF16)
    b_i, b_f = m_igate_b.astype(F32), m_fgate_b.astype(F32)
    return dict(
        ffn1=(row(ffn1_norm), wide(ffn1_w_gate), wide(ffn1_w_up), tall(ffn1_w_down)),
        ffn2=(row(ffn2_norm), wide(ffn2_w_gate), wide(ffn2_w_up), tall(ffn2_w_down)),
        ple=(row(ple_norm), ple_gate_w.astype(BF16), ple_proj_w.astype(BF16)),
        mix_norm=row(mix_norm), w_in=w_all,
        wif=jnp.concatenate([lane_pad(w_i), lane_pad(w_f)], axis=-1),
        wift=jnp.concatenate([sub_pad(jnp.swapaxes(w_i, 1, 2)), sub_pad(jnp.swapaxes(w_f, 1, 2))], axis=1),
        gate_b_col=jnp.concatenate([lane_pad(b_i[:, None, :]), lane_pad(b_f[:, None, :])], axis=-1),
        gate_b_row=jnp.concatenate([sub_pad(b_i[:, :, None]), sub_pad(b_f[:, :, None])], axis=1),
        conv_w=m_conv_w.astype(F32), conv_b=row(m_conv_b), head_sum=_head_sum_matrix(),
        q_gain=row(jnp.tile(a_q_norm, (1, A_Q_HEADS)) * (A_HEAD_DIM ** -0.5 * LOG2E)),
        k_gain=row(jnp.tile(a_k_norm, (1, A_KV_HEADS))),
        out_gain=row(m_out_norm),
        sinks=jnp.broadcast_to((a_sinks.astype(F32) * LOG2E)[:, :, None], (DEPTH, A_Q_HEADS, LANES)),
        wm=w_branch_m.astype(BF16), wa=w_branch_a.astype(BF16), wout=w_out.astype(BF16),
    )


def kernel(x, p, ffn1_norm, ffn1_w_gate, ffn1_w_up, ffn1_w_down, mix_norm, w_in, m_conv_w, m_conv_b, m_igate_b, m_fgate_b, m_out_norm, a_q_norm, a_k_norm, a_sinks, w_branch_m, w_branch_a, w_out, ffn2_norm, ffn2_w_gate, ffn2_w_up, ffn2_w_down, ple_norm, ple_gate_w, ple_proj_w):
    w = _prep(ffn1_norm, ffn1_w_gate, ffn1_w_up, ffn1_w_down, mix_norm, w_in, m_conv_w, m_conv_b, m_igate_b,
              m_fgate_b, m_out_norm, a_q_norm, a_k_norm, a_sinks, w_branch_m, w_branch_a, w_out, ffn2_norm,
              ffn2_w_gate, ffn2_w_up, ffn2_w_down, ple_norm, ple_gate_w, ple_proj_w)
    h = x.reshape(N_TOK, D_MODEL)
    p_tok = p.reshape(DEPTH, N_TOK, PLE_DIM)
    for i in range(DEPTH):
        h = _ffn_call(h, i, *w["ffn1"])
        mx, gcol, grow, ax, sg = _inproj_call(h, i, w)
        hm = _mlstm_call(mx, gcol, grow)
        ha = _swa_call(ax, i, w["sinks"])
        h = _merge_ffn_ple_call(h, hm, ha, sg, i, w["wm"], w["wa"], w["wout"], *w["ffn2"], p_tok, *w["ple"])
    return h.reshape(BATCH, SEQ, D_MODEL)
```
